```python
import jax, jax.numpy as jnp
from jax import lax
import numpy as np

D_MODEL = 1024
BATCH = 8
SEQ = 2048
DEPTH = 1

EPS = 1e-6
MLA_HEADS = 8
MLA_NOPE = 64
MLA_ROPE = 32
MLA_VDIM = 64
MLA_Q_RANK = 384
MLA_KV_RANK = 256
MLA_QK = MLA_NOPE + MLA_ROPE
MLA_WIDTH = MLA_HEADS * MLA_VDIM
ROPE_THETA = 10000.0
Q_BLOCK = 128
GLA_HEADS = 4
GLA_DK = D_MODEL // 2
GLA_DV = D_MODEL
GLA_HK = GLA_DK // GLA_HEADS
GLA_HV = GLA_DV // GLA_HEADS
GLA_GATE_RANK = 16
GLA_GATE_NORM = 16.0
GLA_CHUNK = 64
SPLITS = (MLA_Q_RANK, MLA_KV_RANK, MLA_ROPE, MLA_WIDTH,
          GLA_DK, GLA_DK, GLA_DV, GLA_GATE_RANK, GLA_DV,
          D_MODEL, D_MODEL)
D_IN = sum(SPLITS)

kernel_name = 'hybrid_mla_gla_block'


def rmsnorm(x, g):
    xf = x.astype(jnp.float32)
    y = xf * lax.rsqrt(jnp.mean(xf * xf, axis=-1, keepdims=True) + EPS)
    return (y * g.astype(jnp.float32)).astype(x.dtype)


def rope_tables(positions):
    half = MLA_ROPE // 2
    freqs = ROPE_THETA ** (-jnp.arange(half, dtype=jnp.float32) / half)
    ang = positions.astype(jnp.float32)[..., None] * freqs
    return jnp.cos(ang)[:, :, None, :], jnp.sin(ang)[:, :, None, :]


def apply_rope(x, cos, sin):
    half = MLA_ROPE // 2
    x1 = x[..., :half].astype(jnp.float32)
    x2 = x[..., half:].astype(jnp.float32)
    return jnp.concatenate([x1 * cos - x2 * sin, x2 * cos + x1 * sin], axis=-1).astype(x.dtype)


def mla_attention(q, k, v):
    B, S, H, _ = q.shape
    nb = S // Q_BLOCK
    scale = MLA_QK ** -0.5
    qb = q.reshape(B, nb, Q_BLOCK, H, MLA_QK).transpose(1, 0, 2, 3, 4)
    kpos = jnp.arange(S)

    def block(args):
        i, qi = args
        s = jnp.einsum('bqhd,bkhd->bhqk', qi, k, preferred_element_type=jnp.float32) * scale
        qpos = i * Q_BLOCK + jnp.arange(Q_BLOCK)
        s = jnp.where(kpos[None, :] <= qpos[:, None], s, -jnp.inf)
        p = jax.nn.softmax(s, axis=-1)
        return jnp.einsum('bhqk,bkhd->bqhd', p.astype(v.dtype), v)

    o = lax.map(block, (jnp.arange(nb), qb))
    return o.transpose(1, 0, 2, 3, 4).reshape(B, S, H, MLA_VDIM)


def gla_chunked(q, k, v, log_a):
    B, S, H, DK = q.shape
    DV = v.shape[-1]
    C = GLA_CHUNK
    N = S // C

    def chunks(t):
        return t.reshape(B, N, C, H, t.shape[-1]).transpose(1, 0, 3, 2, 4).astype(jnp.float32)

    qc = chunks(q) * (DK ** -0.5)
    kc, vc, gc = chunks(k), chunks(v), chunks(log_a)
    b = jnp.cumsum(gc, axis=3)
    b_last = b[:, :, :, -1:, :]
    q_in = qc * jnp.exp(b)
    k_in = kc * jnp.exp(-b)
    k_st = kc * jnp.exp(b_last - b)
    causal = jnp.tril(jnp.ones((C, C), jnp.float32))
    attn = jnp.einsum('nbhid,nbhjd->nbhij', q_in, k_in) * causal
    o_intra = jnp.einsum('nbhij,nbhjv->nbhiv', attn, vc)

    def step(state, inp):
        q_i, k_i, v_i, dec = inp
        o = jnp.einsum('bhid,bhdv->bhiv', q_i, state)
        state = dec[:, :, 0, :, None] * state + jnp.einsum('bhjd,bhjv->bhdv', k_i, v_i)
        return state, o

    s0 = jnp.zeros((B, H, DK, DV), jnp.float32)
    _, o_inter = lax.scan(step, s0, (q_in, k_st, vc, jnp.exp(b_last)))
    o = o_intra + o_inter
    return o.transpose(1, 0, 3, 2, 4).reshape(B, S, H, DV)


def setup_inputs(seed: int = 0) -> dict:
    key = jax.random.key(seed)
    ks = jax.random.split(key, 20)

    def w(k, shape, fan_in):
        return jax.random.normal(k, shape, jnp.float32) * fan_in ** -0.5

    def gain(k, shape):
        return 1.0 + 0.02 * jax.random.normal(k, shape, jnp.float32)

    x = jax.random.normal(ks[0], (BATCH, SEQ, D_MODEL), jnp.float32)
    positions = (jnp.arange(SEQ, dtype=jnp.int32)[None, :]
                 + jax.random.randint(ks[1], (BATCH, 1), 0, 64, dtype=jnp.int32))
    return {
        'x': x,
        'positions': positions,
        'g_in': gain(ks[2], (DEPTH, D_MODEL)),
        'w_in': w(ks[3], (DEPTH, D_MODEL, D_IN), D_MODEL),
        'g_q': gain(ks[4], (DEPTH, MLA_Q_RANK)),
        'w_uq': w(ks[5], (DEPTH, MLA_Q_RANK, MLA_HEADS * MLA_QK), MLA_Q_RANK),
        'g_kv': gain(ks[6], (DEPTH, MLA_KV_RANK)),
        'w_ukv': w(ks[7], (DEPTH, MLA_KV_RANK, MLA_HEADS * (MLA_NOPE + MLA_VDIM)), MLA_KV_RANK),
        'w_gla_gate': w(ks[8], (DEPTH, GLA_GATE_RANK, GLA_DK), GLA_GATE_RANK),
        'b_gla_gate': 0.01 * jax.random.normal(ks[9], (DEPTH, GLA_DK), jnp.float32),
        'g_gla': gain(ks[10], (DEPTH, GLA_HV)),
        'w_proj_mla': w(ks[11], (DEPTH, MLA_WIDTH, D_MODEL), MLA_WIDTH),
        'w_proj_gla': w(ks[12], (DEPTH, GLA_DV, D_MODEL), GLA_DV),
        'w_out': w(ks[13], (DEPTH, D_MODEL, D_MODEL), D_MODEL),
        'g_final': gain(ks[14], (D_MODEL,)),
    }


def reference(x, positions, g_in, w_in, g_q, w_uq, g_kv, w_ukv, w_gla_gate, b_gla_gate,
              g_gla, w_proj_mla, w_proj_gla, w_out, g_final):
    B, S, _ = x.shape
    cos, sin = rope_tables(positions)
    split_points = [int(p) for p in np.cumsum(SPLITS)[:-1]]
    for l in range(DEPTH):
        h = rmsnorm(x, g_in[l])
        proj = h @ w_in[l]
        (c_q, c_kv, k_r, z_mla, q_g, k_g, v_g, a_lr, z_gla,
         gate_mla, gate_gla) = jnp.split(proj, split_points, axis=-1)

        q = (rmsnorm(c_q, g_q[l]) @ w_uq[l]).reshape(B, S, MLA_HEADS, MLA_QK)
        q = jnp.concatenate([q[..., :MLA_NOPE], apply_rope(q[..., MLA_NOPE:], cos, sin)], axis=-1)
        kv = (rmsnorm(c_kv, g_kv[l]) @ w_ukv[l]).reshape(B, S, MLA_HEADS, MLA_NOPE + MLA_VDIM)
        k_nope, v_mla = kv[..., :MLA_NOPE], kv[..., MLA_NOPE:]
        k_rope = apply_rope(k_r[:, :, None, :], cos, sin)
        k = jnp.concatenate([k_nope, jnp.broadcast_to(k_rope, (B, S, MLA_HEADS, MLA_ROPE))], axis=-1)
        o_mla = mla_attention(q, k, v_mla).reshape(B, S, MLA_WIDTH)
        y_mla = (o_mla * jax.nn.silu(z_mla)) @ w_proj_mla[l]

        log_a = jax.nn.log_sigmoid((a_lr @ w_gla_gate[l] + b_gla_gate[l]).astype(jnp.float32)) / GLA_GATE_NORM
        o_gla = gla_chunked(q_g.reshape(B, S, GLA_HEADS, GLA_HK),
                            k_g.reshape(B, S, GLA_HEADS, GLA_HK),
                            v_g.reshape(B, S, GLA_HEADS, GLA_HV),
                            log_a.reshape(B, S, GLA_HEADS, GLA_HK))
        o_gla = rmsnorm(o_gla, g_gla[l]).astype(x.dtype).reshape(B, S, GLA_DV)
        y_gla = (o_gla * jax.nn.silu(z_gla)) @ w_proj_gla[l]

        merged = jax.nn.sigmoid(gate_mla) * y_mla + jax.nn.sigmoid(gate_gla) * y_gla
        x = x + merged @ w_out[l]
    return rmsnorm(x, g_final)
```

```python
import functools

import jax
import jax.numpy as jnp
import numpy as np
from jax import lax
from jax.experimental import pallas as pl
from jax.experimental.pallas import tpu as pltpu

D_MODEL = 1024
EPS = 1e-6
MLA_HEADS = 8
MLA_NOPE = 64
MLA_ROPE = 32
MLA_VDIM = 64
MLA_Q_RANK = 384
MLA_KV_RANK = 256
MLA_QK = MLA_NOPE + MLA_ROPE
MLA_WIDTH = MLA_HEADS * MLA_VDIM
ROPE_THETA = 10000.0
GLA_HEADS = 4
GLA_DK = D_MODEL // 2
GLA_DV = D_MODEL
GLA_HK = GLA_DK // GLA_HEADS
GLA_HV = GLA_DV // GLA_HEADS
GLA_GATE_RANK = 16
GLA_GATE_NORM = 16.0
GLA_CHUNK = 64
SPLITS = (MLA_Q_RANK, MLA_KV_RANK, MLA_ROPE, MLA_WIDTH,
          GLA_DK, GLA_DK, GLA_DV, GLA_GATE_RANK, GLA_DV,
          D_MODEL, D_MODEL)

LANES = 128
HEAD_PAD = LANES
ROPE_HALF = MLA_ROPE // 2
ROPE_LO = MLA_NOPE
QPAD = MLA_HEADS * HEAD_PAD

_GROUPS = (("cq", MLA_Q_RANK), ("ckv", MLA_KV_RANK), ("kr", LANES), ("alr", LANES),
           ("zmla", MLA_WIDTH), ("qg", GLA_DK), ("kg", GLA_DK), ("vg", GLA_DV),
           ("zgla", GLA_DV), ("gmla", D_MODEL), ("ggla", D_MODEL))
_OFF = {}
_o = 0
for _n, _w in _GROUPS:
    _OFF[_n] = (_o, _o + _w)
    _o += _w
D_IN_PACKED = _o

VMEM_LIMIT = 52 * 1024 * 1024

BF16 = jnp.bfloat16
F32 = jnp.float32


def _const_spec(shape):
    nd = len(shape)
    return pl.BlockSpec(shape, lambda *_: (0,) * nd, pipeline_mode=pl.Buffered(1))


def _rms(v, g):
    return v * lax.rsqrt(jnp.mean(v * v, axis=-1, keepdims=True) + EPS) * g


def _rope_body(pos_ref, freq_ref, cos_ref, sin_ref):
    ang = pos_ref[...] * freq_ref[...]
    cos_ref[...] = jnp.cos(ang)
    sin_ref[...] = jnp.sin(ang)


def _rope_tables(pos_rows, freq_row):
    rows = pos_rows.shape[0]
    blk = 512
    spec = pl.BlockSpec((blk, LANES), lambda i: (i, 0))
    return pl.pallas_call(
        _rope_body,
        out_shape=(jax.ShapeDtypeStruct((rows, LANES), F32),) * 2,
        grid=(rows // blk,),
        in_specs=[spec, _const_spec((1, LANES))],
        out_specs=(spec, spec),
        name="rope_tables",
    )(pos_rows, freq_row)


def _in_proj_body(x_ref, rc_ref, ra_ref, rb_ref, gin_ref, win_ref, gq_ref, wuq_ref,
                  gkv_ref, wuk_ref, wuv_ref, wgg_ref, bgg_ref,
                  q_ref, k_ref, v_ref, szm_ref, qg_ref, kg_ref, vg_ref, la_ref,
                  szg_ref, gm_ref, gg_ref):
    h = _rms(x_ref[...], gin_ref[...]).astype(BF16)

    def proj(name):
        lo, hi = _OFF[name]
        return jnp.dot(h, win_ref[:, lo:hi], preferred_element_type=F32)

    rc, ra, rb = rc_ref[...], ra_ref[...], rb_ref[...]

    def rope(t):
        return (t * rc + pltpu.roll(t, LANES - ROPE_HALF, 1) * ra
                + pltpu.roll(t, ROPE_HALF, 1) * rb)

    qn = _rms(proj("cq"), gq_ref[...]).astype(BF16)
    q = jnp.dot(qn, wuq_ref[...], preferred_element_type=F32)
    for hd in range(MLA_HEADS):
        sl = slice(hd * HEAD_PAD, (hd + 1) * HEAD_PAD)
        q_ref[:, sl] = rope(q[:, sl]).astype(BF16)

    kvn = _rms(proj("ckv"), gkv_ref[...]).astype(BF16)
    kn = jnp.dot(kvn, wuk_ref[...], preferred_element_type=F32)
    kr = rope(proj("kr"))
    for hd in range(MLA_HEADS):
        sl = slice(hd * HEAD_PAD, (hd + 1) * HEAD_PAD)
        k_ref[:, sl] = (kn[:, sl] + kr).astype(BF16)
    v_ref[...] = jnp.dot(kvn, wuv_ref[...], preferred_element_type=F32).astype(BF16)

    z = proj("zmla")
    szm_ref[...] = (z * jax.nn.sigmoid(z)).astype(BF16)

    qg_ref[...] = proj("qg").astype(BF16)
    kg_ref[...] = proj("kg").astype(BF16)
    vg_ref[...] = proj("vg").astype(BF16)
    alr = proj("alr").astype(BF16)
    zg = jnp.dot(alr, wgg_ref[...], preferred_element_type=F32) + bgg_ref[...]
    log_sig = jnp.minimum(zg, 0.0) - jnp.log1p(jnp.exp(-jnp.abs(zg)))
    la_ref[...] = log_sig * (1.0 / GLA_GATE_NORM)
    z = proj("zgla")
    szg_ref[...] = (z * jax.nn.sigmoid(z)).astype(BF16)

    gm_ref[...] = jax.nn.sigmoid(proj("gmla")).astype(BF16)
    gg_ref[...] = jax.nn.sigmoid(proj("ggla")).astype(BF16)


def _in_proj(x2, rc, ra, rb, gin, win, gq, wuq, gkv, wuk, wuv, wgg, bgg, tm):
    T = x2.shape[0]

    def row(w):
        return pl.BlockSpec((tm, w), lambda i: (i, 0))

    out_widths = (QPAD, QPAD, MLA_WIDTH, MLA_WIDTH, GLA_DK, GLA_DK, GLA_DV, GLA_DK,
                  GLA_DV, D_MODEL, D_MODEL)
    out_dtypes = (BF16,) * 7 + (F32,) + (BF16,) * 3
    return pl.pallas_call(
        _in_proj_body,
        out_shape=tuple(jax.ShapeDtypeStruct((T, w), d) for w, d in zip(out_widths, out_dtypes)),
        grid=(T // tm,),
        in_specs=[row(D_MODEL), row(LANES), row(LANES), row(LANES),
                  _const_spec(gin.shape), _const_spec(win.shape), _const_spec(gq.shape),
                  _const_spec(wuq.shape), _const_spec(gkv.shape), _const_spec(wuk.shape),
                  _const_spec(wuv.shape), _const_spec(wgg.shape), _const_spec(bgg.shape)],
        out_specs=tuple(row(w) for w in out_widths),
        compiler_params=pltpu.CompilerParams(
            dimension_semantics=("arbitrary",), vmem_limit_bytes=VMEM_LIMIT),
        name="in_proj",
    )(x2, rc, ra, rb, gin, win, gq, wuq, gkv, wuk, wuv, wgg, bgg)


def _mla_body(q_ref, k_ref, v_ref, o_ref, *, tq, tk):
    qi = pl.program_id(2)
    heads = 2
    qs = [q_ref[:, hh * HEAD_PAD:(hh + 1) * HEAD_PAD] for hh in range(heads)]

    def step(j, carry, masked):
        ks = pl.ds(pl.multiple_of(j * tk, tk), tk)
        vblk = v_ref[ks, :]
        out = []
        for hh in range(heads):
            m, l, acc = carry[hh]
            kblk = k_ref[ks, hh * HEAD_PAD:(hh + 1) * HEAD_PAD]
            s = lax.dot_general(qs[hh], kblk, (((1,), (1,)), ((), ())),
                                preferred_element_type=F32)
            if masked:
                r = lax.broadcasted_iota(jnp.int32, (tq, tk), 0)
                c = lax.broadcasted_iota(jnp.int32, (tq, tk), 1)
                s = jnp.where(c <= r, s, -jnp.inf)
            m_new = jnp.maximum(m, jnp.max(s, axis=-1, keepdims=True))
            alpha = jnp.exp(m - m_new)
            p = jnp.exp(s - m_new)
            l = alpha * l + jnp.sum(p, axis=-1, keepdims=True)
            acc = alpha * acc + jnp.dot(p.astype(BF16), vblk, preferred_element_type=F32)
            out.append((m_new, l, acc))
        return tuple(out)

    init = tuple((jnp.full((tq, 1), -jnp.inf, F32), jnp.zeros((tq, 1), F32),
                  jnp.zeros((tq, 2 * MLA_VDIM), F32)) for _ in range(heads))
    carry = lax.fori_loop(0, qi, lambda j, c: step(j, c, False), init)
    carry = step(qi, carry, True)
    o0 = carry[0][2] / carry[0][1]
    o1 = carry[1][2] / carry[1][1]
    lane = lax.broadcasted_iota(jnp.int32, o0.shape, 1)
    o_ref[...] = jnp.where(lane < MLA_VDIM, o0, o1).astype(BF16)


def _mla_attn(q3, k3, v3, tq):
    B, S, _ = q3.shape
    pairs = MLA_HEADS // 2
    return pl.pallas_call(
        functools.partial(_mla_body, tq=tq, tk=tq),
        out_shape=jax.ShapeDtypeStruct((B, S, MLA_WIDTH), BF16),
        grid=(B, pairs, S // tq),
        in_specs=[pl.BlockSpec((None, tq, 2 * HEAD_PAD), lambda b, p, i: (b, i, p)),
                  pl.BlockSpec((None, S, 2 * HEAD_PAD), lambda b, p, i: (b, 0, p)),
                  pl.BlockSpec((None, S, 2 * MLA_VDIM), lambda b, p, i: (b, 0, p))],
        out_specs=pl.BlockSpec((None, tq, 2 * MLA_VDIM), lambda b, p, i: (b, i, p)),
        compiler_params=pltpu.CompilerParams(
            dimension_semantics=("arbitrary", "arbitrary", "arbitrary"),
            vmem_limit_bytes=VMEM_LIMIT),
        name="mla_attn",
    )(q3, k3, v3)


def _gla_body(q_ref, k_ref, v_ref, la_ref, g_ref, o_ref, st_ref, *, n_chunks):
    C = GLA_CHUNK
    st_ref[...] = jnp.zeros_like(st_ref)
    r = lax.broadcasted_iota(jnp.int32, (C, C), 0)
    c = lax.broadcasted_iota(jnp.int32, (C, C), 1)
    causal = c <= r
    tril = jnp.where(causal, 1.0, 0.0).astype(BF16)
    gain = g_ref[...]

    def chunk(n, carry):
        rows = pl.ds(pl.multiple_of(n * C, C), C)
        g = la_ref[rows, :]
        g_hi = g.astype(BF16)
        g_lo = (g - g_hi.astype(F32)).astype(BF16)
        b = (jnp.dot(tril, g_hi, preferred_element_type=F32)
             + jnp.dot(tril, g_lo, preferred_element_type=F32))
        b_last = b[C - 1:C, :]
        q = q_ref[rows, :].astype(F32)
        k = k_ref[rows, :].astype(F32)
        v = v_ref[rows, :]
        q_in = (q * jnp.exp(b)).astype(BF16)
        k_in = (k * jnp.exp(-b)).astype(BF16)
        k_st = (k * jnp.exp(b_last - b)).astype(BF16)
        attn = lax.dot_general(q_in, k_in, (((1,), (1,)), ((), ())),
                               preferred_element_type=F32)
        attn = jnp.where(causal, attn, 0.0).astype(BF16)
        st = st_ref[...]
        o = (jnp.dot(attn, v, preferred_element_type=F32)
             + lax.dot_general(q_in, st.astype(BF16), (((1,), (1,)), ((), ())),
                               preferred_element_type=F32))
        st_ref[...] = st * jnp.exp(b_last) + lax.dot_general(
            v, k_st, (((0,), (0,)), ((), ())), preferred_element_type=F32)
        o_ref[rows, :] = _rms(o, gain).astype(BF16)
        return carry

    lax.fori_loop(0, n_chunks, chunk, 0)


def _gla(q3, k3, v3, la3, gain):
    B, S, _ = q3.shape

    def spec(w):
        return pl.BlockSpec((None, S, w), lambda b, h: (b, 0, h))

    return pl.pallas_call(
        functools.partial(_gla_body, n_chunks=S // GLA_CHUNK),
        out_shape=jax.ShapeDtypeStruct((B, S, GLA_DV), BF16),
        grid=(B, GLA_HEADS),
        in_specs=[spec(GLA_HK), spec(GLA_HK), spec(GLA_HV), spec(GLA_HK),
                  _const_spec(gain.shape)],
        out_specs=spec(GLA_HV),
        scratch_shapes=[pltpu.VMEM((GLA_HV, GLA_HK), F32)],
        compiler_params=pltpu.CompilerParams(
            dimension_semantics=("arbitrary", "arbitrary"), vmem_limit_bytes=VMEM_LIMIT),
        name="gla",
    )(q3, k3, v3, la3, gain)


def _out_body(x_ref, om_ref, szm_ref, og_ref, szg_ref, gm_ref, gg_ref,
              wpm_ref, wpg_ref, wo_ref, gf_ref, o_ref, *, final_norm):
    um = (om_ref[...].astype(F32) * szm_ref[...].astype(F32)).astype(BF16)
    ug = (og_ref[...].astype(F32) * szg_ref[...].astype(F32)).astype(BF16)
    y_mla = jnp.dot(um, wpm_ref[...], preferred_element_type=F32)
    y_gla = jnp.dot(ug, wpg_ref[...], preferred_element_type=F32)
    merged = gm_ref[...].astype(F32) * y_mla + gg_ref[...].astype(F32) * y_gla
    r = x_ref[...] + jnp.dot(merged.astype(BF16), wo_ref[...], preferred_element_type=F32)
    o_ref[...] = _rms(r, gf_ref[...]) if final_norm else r


def _out_proj(x2, om, szm, og, szg, gm, gg, wpm, wpg, wo, gf, tm, final_norm):
    T = x2.shape[0]

    def row(w):
        return pl.BlockSpec((tm, w), lambda i: (i, 0))

    return pl.pallas_call(
        functools.partial(_out_body, final_norm=final_norm),
        out_shape=jax.ShapeDtypeStruct((T, D_MODEL), F32),
        grid=(T // tm,),
        in_specs=[row(D_MODEL), row(MLA_WIDTH), row(MLA_WIDTH), row(GLA_DV), row(GLA_DV),
                  row(D_MODEL), row(D_MODEL), _const_spec(wpm.shape), _const_spec(wpg.shape),
                  _const_spec(wo.shape), _const_spec(gf.shape)],
        out_specs=row(D_MODEL),
        compiler_params=pltpu.CompilerParams(
            dimension_semantics=("arbitrary",), vmem_limit_bytes=VMEM_LIMIT),
        name="out_proj",
    )(x2, om, szm, og, szg, gm, gg, wpm, wpg, wo, gf)


def _pack_w_in(w):
    pts = np.cumsum((0,) + SPLITS)
    c = {n: w[:, pts[i]:pts[i + 1]] for i, n in enumerate(
        ("cq", "ckv", "kr", "zmla", "qg", "kg", "vg", "alr", "zgla", "gmla", "ggla"))}
    zeros = lambda n: jnp.zeros((w.shape[0], n), w.dtype)
    c["kr"] = jnp.concatenate([zeros(ROPE_LO), c["kr"], zeros(LANES - ROPE_LO - MLA_ROPE)], 1)
    c["alr"] = jnp.concatenate([c["alr"], zeros(LANES - GLA_GATE_RANK)], 1)
    c["qg"] = c["qg"] * (GLA_HK ** -0.5)
    return jnp.concatenate([c[n] for n, _ in _GROUPS], axis=1).astype(BF16)


def _pad_heads(w, width):
    r = w.shape[0]
    w = w.reshape(r, MLA_HEADS, width)
    w = jnp.pad(w, ((0, 0), (0, 0), (0, HEAD_PAD - width)))
    return w.reshape(r, QPAD)


def kernel(x, positions, g_in, w_in, g_q, w_uq, g_kv, w_ukv, w_gla_gate, b_gla_gate,
           g_gla, w_proj_mla, w_proj_gla, w_out, g_final):
    B, S, D = x.shape
    T = B * S
    depth = w_in.shape[0]
    tm = 256

    freqs = ROPE_THETA ** (-jnp.arange(ROPE_HALF, dtype=F32) / ROPE_HALF)
    tok_per_row = LANES // MLA_ROPE
    pos_rows = jnp.repeat(positions.reshape(T, 1).astype(F32), MLA_ROPE, axis=1)
    pos_rows = pos_rows.reshape(T // tok_per_row, LANES)
    freq_row = jnp.tile(freqs, LANES // ROPE_HALF).reshape(1, LANES)
    cos_r, sin_r = _rope_tables(pos_rows, freq_row)
    cos32 = cos_r.reshape(T, MLA_ROPE)
    sin32 = sin_r.reshape(T, MLA_ROPE)
    tail = LANES - ROPE_LO - MLA_ROPE
    rc = jnp.concatenate([jnp.ones((T, ROPE_LO), F32), cos32, jnp.zeros((T, tail), F32)], 1)
    ra = jnp.concatenate([jnp.zeros((T, ROPE_LO), F32), -sin32[:, :ROPE_HALF],
                          jnp.zeros((T, ROPE_HALF + tail), F32)], 1)
    rb = jnp.concatenate([jnp.zeros((T, ROPE_LO + ROPE_HALF), F32), sin32[:, ROPE_HALF:],
                          jnp.zeros((T, tail), F32)], 1)

    x2 = x.reshape(T, D)
    for l in range(depth):
        win = _pack_w_in(w_in[l])
        wuq = _pad_heads(w_uq[l] * (MLA_QK ** -0.5), MLA_QK).astype(BF16)
        wukv = w_ukv[l].reshape(MLA_KV_RANK, MLA_HEADS, MLA_NOPE + MLA_VDIM)
        wuk = _pad_heads(wukv[:, :, :MLA_NOPE].reshape(MLA_KV_RANK, -1), MLA_NOPE).astype(BF16)
        wuv = wukv[:, :, MLA_NOPE:].reshape(MLA_KV_RANK, MLA_WIDTH).astype(BF16)
        wgg = jnp.pad(w_gla_gate[l], ((0, LANES - GLA_GATE_RANK), (0, 0))).astype(BF16)

        (q, k, v, szm, qg, kg, vg, la, szg, gm, gg) = _in_proj(
            x2, rc, ra, rb, g_in[l].reshape(1, D), win, g_q[l].reshape(1, -1), wuq,
            g_kv[l].reshape(1, -1), wuk, wuv, wgg, b_gla_gate[l].reshape(1, -1), tm)

        o_mla = _mla_attn(q.reshape(B, S, QPAD), k.reshape(B, S, QPAD),
                          v.reshape(B, S, MLA_WIDTH), tq=256)
        o_gla = _gla(qg.reshape(B, S, GLA_DK), kg.reshape(B, S, GLA_DK),
                     vg.reshape(B, S, GLA_DV), la.reshape(B, S, GLA_DK),
                     g_gla[l].reshape(1, GLA_HV))

        x2 = _out_proj(x2, o_mla.reshape(T, MLA_WIDTH), szm, o_gla.reshape(T, GLA_DV), szg,
                       gm, gg, w_proj_mla[l].astype(BF16), w_proj_gla[l].astype(BF16),
                       w_out[l].astype(BF16), g_final.reshape(1, D), tm,
                       final_norm=(l == depth - 1))
    return x2.reshape(B, S, D)
```

```python
import functools

import jax
import jax.numpy as jnp
import numpy as np
from jax import lax
from jax.experimental import pallas as pl
from jax.experimental.pallas import tpu as pltpu

D_MODEL = 1024
EPS = 1e-6
MLA_HEADS = 8
MLA_NOPE = 64
MLA_ROPE = 32
MLA_VDIM = 64
MLA_Q_RANK = 384
MLA_KV_RANK = 256
MLA_QK = MLA_NOPE + MLA_ROPE
MLA_WIDTH = MLA_HEADS * MLA_VDIM
ROPE_THETA = 10000.0
GLA_HEADS = 4
GLA_DK = D_MODEL // 2
GLA_DV = D_MODEL
GLA_HK = GLA_DK // GLA_HEADS
GLA_HV = GLA_DV // GLA_HEADS
GLA_GATE_RANK = 16
GLA_GATE_NORM = 16.0
GLA_CHUNK = 64
SPLITS = (MLA_Q_RANK, MLA_KV_RANK, MLA_ROPE, MLA_WIDTH,
          GLA_DK, GLA_DK, GLA_DV, GLA_GATE_RANK, GLA_DV,
          D_MODEL, D_MODEL)

LANES = 128
SUBLANES = 8
HEAD_PAD = LANES
ROPE_HALF = MLA_ROPE // 2
ROPE_LO = MLA_NOPE
QPAD = MLA_HEADS * HEAD_PAD
LOG2E = 1.4426950408889634

_GROUPS = (("cq", MLA_Q_RANK), ("ckv", MLA_KV_RANK), ("kr", LANES), ("alr", LANES),
           ("qg", GLA_DK), ("kg", GLA_DK), ("vg", GLA_DV),
           ("zgla", GLA_DV), ("gmla", D_MODEL), ("ggla", D_MODEL))
_OFF = {}
_o = 0
for _n, _w in _GROUPS:
    _OFF[_n] = (_o, _o + _w)
    _o += _w

VMEM_LIMIT = 52 * 1024 * 1024

BF16 = jnp.bfloat16
F32 = jnp.float32
_NT = (((1,), (1,)), ((), ()))
_TN = (((0,), (0,)), ((), ()))


def _const_spec(shape):
    nd = len(shape)
    return pl.BlockSpec(shape, lambda *_: (0,) * nd, pipeline_mode=pl.Buffered(1))


def _rms(v, g):
    return v * lax.rsqrt(jnp.mean(v * v, axis=-1, keepdims=True) + EPS) * g


def _rope_body(pos_ref, freq_ref, cos_ref, sin_ref):
    ang = pos_ref[...] * freq_ref[...]
    cos_ref[...] = jnp.cos(ang)
    sin_ref[...] = jnp.sin(ang)


def _rope_tables(pos_rows, freq_row):
    rows = pos_rows.shape[0]
    blk = 512
    spec = pl.BlockSpec((blk, LANES), lambda i: (i, 0))
    return pl.pallas_call(
        _rope_body,
        out_shape=(jax.ShapeDtypeStruct((rows, LANES), F32),) * 2,
        grid=(rows // blk,),
        in_specs=[spec, _const_spec((1, LANES))],
        out_specs=(spec, spec),
        name="rope_tables",
    )(pos_rows, freq_row)


def _in_proj_body(x_ref, rc_ref, ra_ref, rb_ref, gin_ref, win_ref, wzt_ref, gq_ref, wuq_ref,
                  gkv_ref, wuk_ref, wuvt_ref, wgg_ref, bgg_ref,
                  q_ref, k_ref, vt_ref, szmt_ref, qg_ref, kg_ref, vg_ref, la_ref,
                  szg_ref, gm_ref, gg_ref):
    h = _rms(x_ref[...], gin_ref[...]).astype(BF16)

    def proj(name):
        lo, hi = _OFF[name]
        return jnp.dot(h, win_ref[:, lo:hi], preferred_element_type=F32)

    rc, ra, rb = rc_ref[...], ra_ref[...], rb_ref[...]

    def rope(t):
        return (t * rc + pltpu.roll(t, LANES - ROPE_HALF, 1) * ra
                + pltpu.roll(t, ROPE_HALF, 1) * rb)

    qn = _rms(proj("cq"), gq_ref[...]).astype(BF16)
    q = jnp.dot(qn, wuq_ref[...], preferred_element_type=F32)
    for hd in range(MLA_HEADS):
        sl = slice(hd * HEAD_PAD, (hd + 1) * HEAD_PAD)
        q_ref[:, sl] = rope(q[:, sl]).astype(BF16)

    kvn = _rms(proj("ckv"), gkv_ref[...]).astype(BF16)
    kn = jnp.dot(kvn, wuk_ref[...], preferred_element_type=F32)
    kr = rope(proj("kr"))
    for hd in range(MLA_HEADS):
        sl = slice(hd * HEAD_PAD, (hd + 1) * HEAD_PAD)
        k_ref[:, sl] = (kn[:, sl] + kr).astype(BF16)
    vt_ref[...] = lax.dot_general(wuvt_ref[...], kvn, _NT,
                                  preferred_element_type=F32).astype(BF16)
    zt = lax.dot_general(wzt_ref[...], h, _NT, preferred_element_type=F32)
    szmt_ref[...] = (zt * jax.nn.sigmoid(zt)).astype(BF16)

    qg_ref[...] = proj("qg").astype(BF16)
    kg_ref[...] = proj("kg").astype(BF16)
    vg_ref[...] = proj("vg").astype(BF16)
    alr = proj("alr").astype(BF16)
    zg = jnp.dot(alr, wgg_ref[...], preferred_element_type=F32) + bgg_ref[...]
    log_sig = jnp.minimum(zg, 0.0) - jnp.log1p(jnp.exp(-jnp.abs(zg)))
    la_ref[...] = log_sig * (1.0 / GLA_GATE_NORM)
    z = proj("zgla")
    szg_ref[...] = (z * jax.nn.sigmoid(z)).astype(BF16)

    gm_ref[...] = jax.nn.sigmoid(proj("gmla")).astype(BF16)
    gg_ref[...] = jax.nn.sigmoid(proj("ggla")).astype(BF16)


def _in_proj(x2, rc, ra, rb, gin, win, wzt, gq, wuq, gkv, wuk, wuvt, wgg, bgg, tm):
    T = x2.shape[0]

    def row(w):
        return pl.BlockSpec((tm, w), lambda i: (i, 0))

    def colmajor(r):
        return pl.BlockSpec((r, tm), lambda i: (0, i))

    sds = jax.ShapeDtypeStruct
    out_shape = (sds((T, QPAD), BF16), sds((T, QPAD), BF16),
                 sds((MLA_WIDTH, T), BF16), sds((MLA_WIDTH, T), BF16),
                 sds((T, GLA_DK), BF16), sds((T, GLA_DK), BF16), sds((T, GLA_DV), BF16),
                 sds((T, GLA_DK), F32), sds((T, GLA_DV), BF16),
                 sds((T, D_MODEL), BF16), sds((T, D_MODEL), BF16))
    out_specs = (row(QPAD), row(QPAD), colmajor(MLA_WIDTH), colmajor(MLA_WIDTH),
                 row(GLA_DK), row(GLA_DK), row(GLA_DV), row(GLA_DK), row(GLA_DV),
                 row(D_MODEL), row(D_MODEL))
    consts = (gin, win, wzt, gq, wuq, gkv, wuk, wuvt, wgg, bgg)
    return pl.pallas_call(
        _in_proj_body,
        out_shape=out_shape,
        grid=(T // tm,),
        in_specs=[row(D_MODEL), row(LANES), row(LANES), row(LANES)]
        + [_const_spec(c.shape) for c in consts],
        out_specs=out_specs,
        compiler_params=pltpu.CompilerParams(
            dimension_semantics=("arbitrary",), vmem_limit_bytes=VMEM_LIMIT),
        name="in_proj",
    )(x2, rc, ra, rb, *consts)


def _fold_rows(t, op):
    return op(t.reshape(t.shape[0] // SUBLANES, SUBLANES, t.shape[1]), axis=0)


def _mla_body(q_ref, k_ref, vt_ref, o_ref, s_ref, p_ref, *, tq, nq):
    heads = 2
    row = lax.broadcasted_iota(jnp.int32, (tq, tq), 0)
    col = lax.broadcasted_iota(jnp.int32, (tq, tq), 1)
    keep = row <= col
    unit = 0
    for qi in range(nq):
        qrows = slice(qi * tq, (qi + 1) * tq)
        kv_len = (qi + 1) * tq
        for hh in range(heads):
            buf = unit % 2
            unit += 1
            lanes = slice(hh * HEAD_PAD, (hh + 1) * HEAD_PAD)
            qh = q_ref[qrows, lanes]
            m8 = None
            for t in range(qi + 1):
                krows = slice(t * tq, (t + 1) * tq)
                s = lax.dot_general(k_ref[krows, lanes], qh, _NT,
                                    preferred_element_type=F32)
                if t == qi:
                    s = jnp.where(keep, s, -jnp.inf)
                s_ref[buf, krows, :] = s
                tile_max = _fold_rows(s, jnp.max)
                m8 = tile_max if m8 is None else jnp.maximum(m8, tile_max)
            m = jnp.max(m8, axis=0, keepdims=True)
            l8 = jnp.zeros((SUBLANES, tq), F32)
            for t in range(qi + 1):
                krows = slice(t * tq, (t + 1) * tq)
                p = jnp.exp2(s_ref[buf, krows, :] - m)
                l8 = l8 + _fold_rows(p, jnp.sum)
                p_ref[buf, krows, :] = p.astype(BF16)
            l = jnp.sum(l8, axis=0, keepdims=True)
            vrows = slice(hh * MLA_VDIM, (hh + 1) * MLA_VDIM)
            o_t = jnp.dot(vt_ref[vrows, 0:kv_len], p_ref[buf, 0:kv_len, :],
                          preferred_element_type=F32)
            o_ref[vrows, qrows] = (o_t / l).astype(BF16)


def _mla_attn(q2, k2, vt, B, S, tq):
    pairs = MLA_HEADS // 2
    return pl.pallas_call(
        functools.partial(_mla_body, tq=tq, nq=S // tq),
        out_shape=jax.ShapeDtypeStruct((MLA_WIDTH, B * S), BF16),
        grid=(B, pairs),
        in_specs=[pl.BlockSpec((S, 2 * HEAD_PAD), lambda b, p: (b, p)),
                  pl.BlockSpec((S, 2 * HEAD_PAD), lambda b, p: (b, p)),
                  pl.BlockSpec((2 * MLA_VDIM, S), lambda b, p: (p, b))],
        out_specs=pl.BlockSpec((2 * MLA_VDIM, S), lambda b, p: (p, b)),
        scratch_shapes=[pltpu.VMEM((2, S, tq), F32), pltpu.VMEM((2, S, tq), BF16)],
        compiler_params=pltpu.CompilerParams(
            dimension_semantics=("arbitrary", "arbitrary"), vmem_limit_bytes=VMEM_LIMIT),
        name="mla_attn",
    )(q2, k2, vt)


def _gla_body(q_ref, k_ref, v_ref, la_ref, g_ref, o_ref, st_ref, *, n_chunks):
    C = GLA_CHUNK
    st_ref[...] = jnp.zeros_like(st_ref)
    r = lax.broadcasted_iota(jnp.int32, (C, C), 0)
    c = lax.broadcasted_iota(jnp.int32, (C, C), 1)
    causal = c <= r
    tril = jnp.where(causal, 1.0, 0.0).astype(BF16)
    gain = g_ref[...]

    def chunk(n, carry):
        rows = pl.ds(pl.multiple_of(n * C, C), C)
        g = la_ref[rows, :]
        g_hi = g.astype(BF16)
        g_lo = (g - g_hi.astype(F32)).astype(BF16)
        b = (jnp.dot(tril, g_hi, preferred_element_type=F32)
             + jnp.dot(tril, g_lo, preferred_element_type=F32))
        b_last = b[C - 1:C, :]
        q = q_ref[rows, :].astype(F32)
        k = k_ref[rows, :].astype(F32)
        v = v_ref[rows, :]
        q_in = (q * jnp.exp(b)).astype(BF16)
        k_in = (k * jnp.exp(-b)).astype(BF16)
        k_st = (k * jnp.exp(b_last - b)).astype(BF16)
        attn = lax.dot_general(q_in, k_in, _NT, preferred_element_type=F32)
        attn = jnp.where(causal, attn, 0.0).astype(BF16)
        st = st_ref[...]
        o = (jnp.dot(attn, v, preferred_element_type=F32)
             + lax.dot_general(q_in, st.astype(BF16), _NT, preferred_element_type=F32))
        st_ref[...] = st * jnp.exp(b_last) + lax.dot_general(
            v, k_st, _TN, preferred_element_type=F32)
        o_ref[rows, :] = _rms(o, gain).astype(BF16)
        return carry

    lax.fori_loop(0, n_chunks, chunk, 0)


def _gla(q3, k3, v3, la3, gain):
    B, S, _ = q3.shape

    def spec(w):
        return pl.BlockSpec((None, S, w), lambda b, h: (b, 0, h))

    return pl.pallas_call(
        functools.partial(_gla_body, n_chunks=S // GLA_CHUNK),
        out_shape=jax.ShapeDtypeStruct((B, S, GLA_DV), BF16),
        grid=(B, GLA_HEADS),
        in_specs=[spec(GLA_HK), spec(GLA_HK), spec(GLA_HV), spec(GLA_HK),
                  _const_spec(gain.shape)],
        out_specs=spec(GLA_HV),
        scratch_shapes=[pltpu.VMEM((GLA_HV, GLA_HK), F32)],
        compiler_params=pltpu.CompilerParams(
            dimension_semantics=("arbitrary", "arbitrary"), vmem_limit_bytes=VMEM_LIMIT),
        name="gla",
    )(q3, k3, v3, la3, gain)


def _out_body(x_ref, omt_ref, szmt_ref, og_ref, szg_ref, gm_ref, gg_ref,
              wpm_ref, wpg_ref, wo_ref, gf_ref, o_ref, *, final_norm):
    umt = (omt_ref[...].astype(F32) * szmt_ref[...].astype(F32)).astype(BF16)
    ug = (og_ref[...].astype(F32) * szg_ref[...].astype(F32)).astype(BF16)
    y_mla = lax.dot_general(umt, wpm_ref[...], _TN, preferred_element_type=F32)
    y_gla = jnp.dot(ug, wpg_ref[...], preferred_element_type=F32)
    merged = gm_ref[...].astype(F32) * y_mla + gg_ref[...].astype(F32) * y_gla
    r = x_ref[...] + jnp.dot(merged.astype(BF16), wo_ref[...], preferred_element_type=F32)
    o_ref[...] = _rms(r, gf_ref[...]) if final_norm else r


def _out_proj(x2, omt, szmt, og, szg, gm, gg, wpm, wpg, wo, gf, tm, final_norm):
    T = x2.shape[0]

    def row(w):
        return pl.BlockSpec((tm, w), lambda i: (i, 0))

    def colmajor(r):
        return pl.BlockSpec((r, tm), lambda i: (0, i))

    return pl.pallas_call(
        functools.partial(_out_body, final_norm=final_norm),
        out_shape=jax.ShapeDtypeStruct((T, D_MODEL), F32),
        grid=(T // tm,),
        in_specs=[row(D_MODEL), colmajor(MLA_WIDTH), colmajor(MLA_WIDTH), row(GLA_DV),
                  row(GLA_DV), row(D_MODEL), row(D_MODEL), _const_spec(wpm.shape),
                  _const_spec(wpg.shape), _const_spec(wo.shape), _const_spec(gf.shape)],
        out_specs=row(D_MODEL),
        compiler_params=pltpu.CompilerParams(
            dimension_semantics=("arbitrary",), vmem_limit_bytes=VMEM_LIMIT),
        name="out_proj",
    )(x2, omt, szmt, og, szg, gm, gg, wpm, wpg, wo, gf)


def _split_w_in(w):
    pts = np.cumsum((0,) + SPLITS)
    names = ("cq", "ckv", "kr", "zmla", "qg", "kg", "vg", "alr", "zgla", "gmla", "ggla")
    return {n: w[:, pts[i]:pts[i + 1]] for i, n in enumerate(names)}


def _pack_w_in(c):
    rows = c["cq"].shape[0]
    zeros = lambda n: jnp.zeros((rows, n), c["cq"].dtype)
    c = dict(c)
    c["kr"] = jnp.concatenate([zeros(ROPE_LO), c["kr"], zeros(LANES - ROPE_LO - MLA_ROPE)], 1)
    c["alr"] = jnp.concatenate([c["alr"], zeros(LANES - GLA_GATE_RANK)], 1)
    c["qg"] = c["qg"] * (GLA_HK ** -0.5)
    return jnp.concatenate([c[n] for n, _ in _GROUPS], axis=1).astype(BF16)


def _pad_heads(w, width):
    r = w.shape[0]
    w = w.reshape(r, MLA_HEADS, width)
    w = jnp.pad(w, ((0, 0), (0, 0), (0, HEAD_PAD - width)))
    return w.reshape(r, QPAD)


def kernel(x, positions, g_in, w_in, g_q, w_uq, g_kv, w_ukv, w_gla_gate, b_gla_gate,
           g_gla, w_proj_mla, w_proj_gla, w_out, g_final):
    B, S, D = x.shape
    T = B * S
    depth = w_in.shape[0]
    tm = 256

    freqs = ROPE_THETA ** (-jnp.arange(ROPE_HALF, dtype=F32) / ROPE_HALF)
    tok_per_row = LANES // MLA_ROPE
    pos_rows = jnp.repeat(positions.reshape(T, 1).astype(F32), MLA_ROPE, axis=1)
    pos_rows = pos_rows.reshape(T // tok_per_row, LANES)
    freq_row = jnp.tile(freqs, LANES // ROPE_HALF).reshape(1, LANES)
    cos_r, sin_r = _rope_tables(pos_rows, freq_row)
    cos32 = cos_r.reshape(T, MLA_ROPE)
    sin32 = sin_r.reshape(T, MLA_ROPE)
    tail = LANES - ROPE_LO - MLA_ROPE
    rc = jnp.concatenate([jnp.ones((T, ROPE_LO), F32), cos32, jnp.zeros((T, tail), F32)], 1)
    ra = jnp.concatenate([jnp.zeros((T, ROPE_LO), F32), -sin32[:, :ROPE_HALF],
                          jnp.zeros((T, ROPE_HALF + tail), F32)], 1)
    rb = jnp.concatenate([jnp.zeros((T, ROPE_LO + ROPE_HALF), F32), sin32[:, ROPE_HALF:],
                          jnp.zeros((T, tail), F32)], 1)

    x2 = x.reshape(T, D)
    for l in range(depth):
        cols = _split_w_in(w_in[l])
        win = _pack_w_in(cols)
        wzt = cols["zmla"].T.astype(BF16)
        wuq = _pad_heads(w_uq[l] * (MLA_QK ** -0.5 * LOG2E), MLA_QK).astype(BF16)
        wukv = w_ukv[l].reshape(MLA_KV_RANK, MLA_HEADS, MLA_NOPE + MLA_VDIM)
        wuk = _pad_heads(wukv[:, :, :MLA_NOPE].reshape(MLA_KV_RANK, -1), MLA_NOPE).astype(BF16)
        wuvt = wukv[:, :, MLA_NOPE:].reshape(MLA_KV_RANK, MLA_WIDTH).T.astype(BF16)
        wgg = jnp.pad(w_gla_gate[l], ((0, LANES - GLA_GATE_RANK), (0, 0))).astype(BF16)

        (q, k, vt, szmt, qg, kg, vg, la, szg, gm, gg) = _in_proj(
            x2, rc, ra, rb, g_in[l].reshape(1, D), win, wzt, g_q[l].reshape(1, -1), wuq,
            g_kv[l].reshape(1, -1), wuk, wuvt, wgg, b_gla_gate[l].reshape(1, -1), tm)

        omt = _mla_attn(q, k, vt, B, S, tq=256)
        o_gla = _gla(qg.reshape(B, S, GLA_DK), kg.reshape(B, S, GLA_DK),
                     vg.reshape(B, S, GLA_DV), la.reshape(B, S, GLA_DK),
                     g_gla[l].reshape(1, GLA_HV))

        x2 = _out_proj(x2, omt, szmt, o_gla.reshape(T, GLA_DV), szg,
                       gm, gg, w_proj_mla[l].astype(BF16), w_proj_gla[l].astype(BF16),
                       w_out[l].astype(BF16), g_final.reshape(1, D), tm,
                       final_norm=(l == depth - 1))
    return x2.reshape(B, S, D)
```

```python
import functools

import jax
import jax.numpy as jnp
import numpy as np
from jax import lax
from jax.experimental import pallas as pl
from jax.experimental.pallas import tpu as pltpu

D_MODEL = 1024
EPS = 1e-6
MLA_HEADS = 8
MLA_NOPE = 64
MLA_ROPE = 32
MLA_VDIM = 64
MLA_Q_RANK = 384
MLA_KV_RANK = 256
MLA_QK = MLA_NOPE + MLA_ROPE
MLA_WIDTH = MLA_HEADS * MLA_VDIM
ROPE_THETA = 10000.0
GLA_HEADS = 4
GLA_DK = D_MODEL // 2
GLA_DV = D_MODEL
GLA_HK = GLA_DK // GLA_HEADS
GLA_HV = GLA_DV // GLA_HEADS
GLA_GATE_RANK = 16
GLA_GATE_NORM = 16.0
GLA_CHUNK = 64
SPLITS = (MLA_Q_RANK, MLA_KV_RANK, MLA_ROPE, MLA_WIDTH,
          GLA_DK, GLA_DK, GLA_DV, GLA_GATE_RANK, GLA_DV,
          D_MODEL, D_MODEL)

LANES = 128
SUBLANES = 8
HEAD_PAD = LANES
ROPE_HALF = MLA_ROPE // 2
ROPE_LO = MLA_NOPE
QPAD = MLA_HEADS * HEAD_PAD
LOG2E = 1.4426950408889634

_GROUPS = (("cq", MLA_Q_RANK), ("ckv", MLA_KV_RANK), ("kr", LANES), ("alr", LANES),
           ("qg", GLA_DK), ("kg", GLA_DK), ("vg", GLA_DV),
           ("zgla", GLA_DV), ("gmla", D_MODEL), ("ggla", D_MODEL))
_OFF = {}
_o = 0
for _n, _w in _GROUPS:
    _OFF[_n] = (_o, _o + _w)
    _o += _w

VMEM_LIMIT = 52 * 1024 * 1024

BF16 = jnp.bfloat16
F32 = jnp.float32
_NT = (((1,), (1,)), ((), ()))
_TN = (((0,), (0,)), ((), ()))


def _const_spec(shape):
    nd = len(shape)
    return pl.BlockSpec(shape, lambda *_: (0,) * nd, pipeline_mode=pl.Buffered(1))


def _rms(v, g):
    return v * lax.rsqrt(jnp.mean(v * v, axis=-1, keepdims=True) + EPS) * g


def _rope_body(pos_ref, freq_ref, cos_ref, sin_ref):
    ang = pos_ref[...] * freq_ref[...]
    cos_ref[...] = jnp.cos(ang)
    sin_ref[...] = jnp.sin(ang)


def _rope_tables(pos_rows, freq_row):
    rows = pos_rows.shape[0]
    blk = 512
    spec = pl.BlockSpec((blk, LANES), lambda i: (i, 0))
    return pl.pallas_call(
        _rope_body,
        out_shape=(jax.ShapeDtypeStruct((rows, LANES), F32),) * 2,
        grid=(rows // blk,),
        in_specs=[spec, _const_spec((1, LANES))],
        out_specs=(spec, spec),
        name="rope_tables",
    )(pos_rows, freq_row)


def _in_proj_body(x_ref, rc_ref, ra_ref, rb_ref, gin_ref, win_ref, wzt_ref, gq_ref, wuq_ref,
                  gkv_ref, wuk_ref, wuvt_ref, wgg_ref, bgg_ref,
                  q_ref, k_ref, vt_ref, szmt_ref, qg_ref, kg_ref, vg_ref, la_ref,
                  szg_ref, gm_ref, gg_ref):
    h = _rms(x_ref[...], gin_ref[...]).astype(BF16)

    def proj(name):
        lo, hi = _OFF[name]
        return jnp.dot(h, win_ref[:, lo:hi], preferred_element_type=F32)

    rc, ra, rb = rc_ref[...], ra_ref[...], rb_ref[...]

    def rope(t):
        return (t * rc + pltpu.roll(t, LANES - ROPE_HALF, 1) * ra
                + pltpu.roll(t, ROPE_HALF, 1) * rb)

    qn = _rms(proj("cq"), gq_ref[...]).astype(BF16)
    q = jnp.dot(qn, wuq_ref[...], preferred_element_type=F32)
    for hd in range(MLA_HEADS):
        sl = slice(hd * HEAD_PAD, (hd + 1) * HEAD_PAD)
        q_ref[:, sl] = rope(q[:, sl]).astype(BF16)

    kvn = _rms(proj("ckv"), gkv_ref[...]).astype(BF16)
    kn = jnp.dot(kvn, wuk_ref[...], preferred_element_type=F32)
    kr = rope(proj("kr"))
    for hd in range(MLA_HEADS):
        sl = slice(hd * HEAD_PAD, (hd + 1) * HEAD_PAD)
        k_ref[:, sl] = (kn[:, sl] + kr).astype(BF16)
    vt_ref[...] = lax.dot_general(wuvt_ref[...], kvn, _NT,
                                  preferred_element_type=F32).astype(BF16)
    zt = lax.dot_general(wzt_ref[...], h, _NT, preferred_element_type=F32)
    szmt_ref[...] = (zt * jax.nn.sigmoid(zt)).astype(BF16)

    qg_ref[...] = proj("qg").astype(BF16)
    kg_ref[...] = proj("kg").astype(BF16)
    vg_ref[...] = proj("vg").astype(BF16)
    alr = proj("alr").astype(BF16)
    zg = jnp.dot(alr, wgg_ref[...], preferred_element_type=F32) + bgg_ref[...]
    log_sig = jnp.minimum(zg, 0.0) - jnp.log1p(jnp.exp(-jnp.abs(zg)))
    la_ref[...] = log_sig * (1.0 / GLA_GATE_NORM)
    z = proj("zgla")
    szg_ref[...] = (z * jax.nn.sigmoid(z)).astype(BF16)

    gm_ref[...] = jax.nn.sigmoid(proj("gmla")).astype(BF16)
    gg_ref[...] = jax.nn.sigmoid(proj("ggla")).astype(BF16)


def _in_proj(x2, rc, ra, rb, gin, win, wzt, gq, wuq, gkv, wuk, wuvt, wgg, bgg, tm):
    T = x2.shape[0]

    def row(w):
        return pl.BlockSpec((tm, w), lambda i: (i, 0))

    def colmajor(r):
        return pl.BlockSpec((r, tm), lambda i: (0, i))

    sds = jax.ShapeDtypeStruct
    out_shape = (sds((T, QPAD), BF16), sds((T, QPAD), BF16),
                 sds((MLA_WIDTH, T), BF16), sds((MLA_WIDTH, T), BF16),
                 sds((T, GLA_DK), BF16), sds((T, GLA_DK), BF16), sds((T, GLA_DV), BF16),
                 sds((T, GLA_DK), F32), sds((T, GLA_DV), BF16),
                 sds((T, D_MODEL), BF16), sds((T, D_MODEL), BF16))
    out_specs = (row(QPAD), row(QPAD), colmajor(MLA_WIDTH), colmajor(MLA_WIDTH),
                 row(GLA_DK), row(GLA_DK), row(GLA_DV), row(GLA_DK), row(GLA_DV),
                 row(D_MODEL), row(D_MODEL))
    consts = (gin, win, wzt, gq, wuq, gkv, wuk, wuvt, wgg, bgg)
    return pl.pallas_call(
        _in_proj_body,
        out_shape=out_shape,
        grid=(T // tm,),
        in_specs=[row(D_MODEL), row(LANES), row(LANES), row(LANES)]
        + [_const_spec(c.shape) for c in consts],
        out_specs=out_specs,
        compiler_params=pltpu.CompilerParams(
            dimension_semantics=("arbitrary",), vmem_limit_bytes=VMEM_LIMIT),
        name="in_proj",
    )(x2, rc, ra, rb, *consts)


def _fold_rows(t, op):
    return op(t.reshape(t.shape[0] // SUBLANES, SUBLANES, t.shape[1]), axis=0)


def _mla_body(q_ref, k_ref, vt_ref, o_ref, s_ref, p_ref, *, tq, nq):
    heads = 2
    row = lax.broadcasted_iota(jnp.int32, (tq, tq), 0)
    col = lax.broadcasted_iota(jnp.int32, (tq, tq), 1)
    keep = row <= col
    unit = 0
    for qi in range(nq):
        qrows = slice(qi * tq, (qi + 1) * tq)
        kv_len = (qi + 1) * tq
        for hh in range(heads):
            buf = unit % 2
            unit += 1
            lanes = slice(hh * HEAD_PAD, (hh + 1) * HEAD_PAD)
            qh = q_ref[qrows, lanes]
            m8 = None
            for t in range(qi + 1):
                krows = slice(t * tq, (t + 1) * tq)
                s = lax.dot_general(k_ref[krows, lanes], qh, _NT,
                                    preferred_element_type=F32)
                if t == qi:
                    s = jnp.where(keep, s, -jnp.inf)
                s_ref[buf, krows, :] = s
                tile_max = _fold_rows(s, jnp.max)
                m8 = tile_max if m8 is None else jnp.maximum(m8, tile_max)
            m = jnp.max(m8, axis=0, keepdims=True)
            l8 = jnp.zeros((SUBLANES, tq), F32)
            for t in range(qi + 1):
                krows = slice(t * tq, (t + 1) * tq)
                p = jnp.exp2(s_ref[buf, krows, :] - m)
                l8 = l8 + _fold_rows(p, jnp.sum)
                p_ref[buf, krows, :] = p.astype(BF16)
            l = jnp.sum(l8, axis=0, keepdims=True)
            vrows = slice(hh * MLA_VDIM, (hh + 1) * MLA_VDIM)
            o_t = jnp.dot(vt_ref[vrows, 0:kv_len], p_ref[buf, 0:kv_len, :],
                          preferred_element_type=F32)
            o_ref[vrows, qrows] = (o_t / l).astype(BF16)


def _mla_attn(q2, k2, vt, B, S, tq):
    pairs = MLA_HEADS // 2
    return pl.pallas_call(
        functools.partial(_mla_body, tq=tq, nq=S // tq),
        out_shape=jax.ShapeDtypeStruct((MLA_WIDTH, B * S), BF16),
        grid=(B, pairs),
        in_specs=[pl.BlockSpec((S, 2 * HEAD_PAD), lambda b, p: (b, p)),
                  pl.BlockSpec((S, 2 * HEAD_PAD), lambda b, p: (b, p)),
                  pl.BlockSpec((2 * MLA_VDIM, S), lambda b, p: (p, b))],
        out_specs=pl.BlockSpec((2 * MLA_VDIM, S), lambda b, p: (p, b)),
        scratch_shapes=[pltpu.VMEM((2, S, tq), F32), pltpu.VMEM((2, S, tq), BF16)],
        compiler_params=pltpu.CompilerParams(
            dimension_semantics=("arbitrary", "arbitrary"), vmem_limit_bytes=VMEM_LIMIT),
        name="mla_attn",
    )(q2, k2, vt)


GLA_BLOCK = 256


def _gla_body(q_ref, k_ref, v_ref, la_ref, g_ref, o_ref, oi_ref, qin_ref, u_ref, dec_ref,
              *, seq):
    C, BLK = GLA_CHUNK, GLA_BLOCK
    per_blk = BLK // C
    r = lax.broadcasted_iota(jnp.int32, (BLK, BLK), 0)
    c = lax.broadcasted_iota(jnp.int32, (BLK, BLK), 1)
    causal = (c <= r) & (r // C == c // C)
    tril = jnp.where(causal, 1.0, 0.0).astype(BF16)
    gain = g_ref[...]

    for blk in range(seq // BLK):
        rows = slice(blk * BLK, (blk + 1) * BLK)
        g = la_ref[rows, :]
        g_hi = g.astype(BF16)
        g_lo = (g - g_hi.astype(F32)).astype(BF16)
        bb = jnp.dot(tril, jnp.concatenate([g_hi, g_lo], axis=1), preferred_element_type=F32)
        b = bb[:, :GLA_HK] + bb[:, GLA_HK:]
        lasts = [b[(j + 1) * C - 1:(j + 1) * C, :] for j in range(per_blk)]
        b_last = jnp.concatenate([jnp.broadcast_to(t, (C, GLA_HK)) for t in lasts], axis=0)
        q = q_ref[rows, :].astype(F32)
        k = k_ref[rows, :].astype(F32)
        q_in = (q * jnp.exp(b)).astype(BF16)
        k_in = (k * jnp.exp(-b)).astype(BF16)
        k_st = (k * jnp.exp(b_last - b)).astype(BF16)
        attn = lax.dot_general(q_in, k_in, _NT, preferred_element_type=F32)
        attn = jnp.where(causal, attn, 0.0).astype(BF16)
        oi_ref[rows, :] = jnp.dot(attn, v_ref[rows, :], preferred_element_type=F32)
        qin_ref[rows, :] = q_in
        for j in range(per_blk):
            n = blk * per_blk + j
            crow = slice(n * C, (n + 1) * C)
            u_ref[n] = lax.dot_general(v_ref[crow, :], k_st[j * C:(j + 1) * C, :], _TN,
                                       preferred_element_type=F32)
            dec_ref[n] = jnp.exp(lasts[j])

    st = jnp.zeros((GLA_HV, GLA_HK), F32)
    for n in range(seq // C):
        crow = slice(n * C, (n + 1) * C)
        o = oi_ref[crow, :] + lax.dot_general(qin_ref[crow, :], st.astype(BF16), _NT,
                                              preferred_element_type=F32)
        o_ref[crow, :] = _rms(o, gain).astype(BF16)
        st = st * dec_ref[n] + u_ref[n]


def _gla(q3, k3, v3, la3, gain):
    B, S, _ = q3.shape
    n_chunks = S // GLA_CHUNK

    def spec(w):
        return pl.BlockSpec((None, S, w), lambda b, h: (b, 0, h))

    return pl.pallas_call(
        functools.partial(_gla_body, seq=S),
        out_shape=jax.ShapeDtypeStruct((B, S, GLA_DV), BF16),
        grid=(B, GLA_HEADS),
        in_specs=[spec(GLA_HK), spec(GLA_HK), spec(GLA_HV), spec(GLA_HK),
                  _const_spec(gain.shape)],
        out_specs=spec(GLA_HV),
        scratch_shapes=[pltpu.VMEM((S, GLA_HV), F32), pltpu.VMEM((S, GLA_HK), BF16),
                        pltpu.VMEM((n_chunks, GLA_HV, GLA_HK), F32),
                        pltpu.VMEM((n_chunks, 1, GLA_HK), F32)],
        compiler_params=pltpu.CompilerParams(
            dimension_semantics=("arbitrary", "arbitrary"), vmem_limit_bytes=VMEM_LIMIT),
        name="gla",
    )(q3, k3, v3, la3, gain)


def _out_body(x_ref, omt_ref, szmt_ref, og_ref, szg_ref, gm_ref, gg_ref,
              wpm_ref, wpg_ref, wo_ref, gf_ref, o_ref, *, final_norm):
    umt = (omt_ref[...].astype(F32) * szmt_ref[...].astype(F32)).astype(BF16)
    ug = (og_ref[...].astype(F32) * szg_ref[...].astype(F32)).astype(BF16)
    y_mla = lax.dot_general(umt, wpm_ref[...], _TN, preferred_element_type=F32)
    y_gla = jnp.dot(ug, wpg_ref[...], preferred_element_type=F32)
    merged = gm_ref[...].astype(F32) * y_mla + gg_ref[...].astype(F32) * y_gla
    r = x_ref[...] + jnp.dot(merged.astype(BF16), wo_ref[...], preferred_element_type=F32)
    o_ref[...] = _rms(r, gf_ref[...]) if final_norm else r


def _out_proj(x2, omt, szmt, og, szg, gm, gg, wpm, wpg, wo, gf, tm, final_norm):
    T = x2.shape[0]

    def row(w):
        return pl.BlockSpec((tm, w), lambda i: (i, 0))

    def colmajor(r):
        return pl.BlockSpec((r, tm), lambda i: (0, i))

    return pl.pallas_call(
        functools.partial(_out_body, final_norm=final_norm),
        out_shape=jax.ShapeDtypeStruct((T, D_MODEL), F32),
        grid=(T // tm,),
        in_specs=[row(D_MODEL), colmajor(MLA_WIDTH), colmajor(MLA_WIDTH), row(GLA_DV),
                  row(GLA_DV), row(D_MODEL), row(D_MODEL), _const_spec(wpm.shape),
                  _const_spec(wpg.shape), _const_spec(wo.shape), _const_spec(gf.shape)],
        out_specs=row(D_MODEL),
        compiler_params=pltpu.CompilerParams(
            dimension_semantics=("arbitrary",), vmem_limit_bytes=VMEM_LIMIT),
        name="out_proj",
    )(x2, omt, szmt, og, szg, gm, gg, wpm, wpg, wo, gf)


def _split_w_in(w):
    pts = np.cumsum((0,) + SPLITS)
    names = ("cq", "ckv", "kr", "zmla", "qg", "kg", "vg", "alr", "zgla", "gmla", "ggla")
    return {n: w[:, pts[i]:pts[i + 1]] for i, n in enumerate(names)}


def _pack_w_in(c):
    rows = c["cq"].shape[0]
    zeros = lambda n: jnp.zeros((rows, n), c["cq"].dtype)
    c = dict(c)
    c["kr"] = jnp.concatenate([zeros(ROPE_LO), c["kr"], zeros(LANES - ROPE_LO - MLA_ROPE)], 1)
    c["alr"] = jnp.concatenate([c["alr"], zeros(LANES - GLA_GATE_RANK)], 1)
    c["qg"] = c["qg"] * (GLA_HK ** -0.5)
    return jnp.concatenate([c[n] for n, _ in _GROUPS], axis=1).astype(BF16)


def _pad_heads(w, width):
    r = w.shape[0]
    w = w.reshape(r, MLA_HEADS, width)
    w = jnp.pad(w, ((0, 0), (0, 0), (0, HEAD_PAD - width)))
    return w.reshape(r, QPAD)


def kernel(x, positions, g_in, w_in, g_q, w_uq, g_kv, w_ukv, w_gla_gate, b_gla_gate,
           g_gla, w_proj_mla, w_proj_gla, w_out, g_final):
    B, S, D = x.shape
    T = B * S
    depth = w_in.shape[0]
    tm = 512

    freqs = ROPE_THETA ** (-jnp.arange(ROPE_HALF, dtype=F32) / ROPE_HALF)
    tok_per_row = LANES // MLA_ROPE
    pos_rows = jnp.repeat(positions.reshape(T, 1).astype(F32), MLA_ROPE, axis=1)
    pos_rows = pos_rows.reshape(T // tok_per_row, LANES)
    freq_row = jnp.tile(freqs, LANES // ROPE_HALF).reshape(1, LANES)
    cos_r, sin_r = _rope_tables(pos_rows, freq_row)
    cos32 = cos_r.reshape(T, MLA_ROPE)
    sin32 = sin_r.reshape(T, MLA_ROPE)
    tail = LANES - ROPE_LO - MLA_ROPE
    rc = jnp.concatenate([jnp.ones((T, ROPE_LO), F32), cos32, jnp.zeros((T, tail), F32)], 1)
    ra = jnp.concatenate([jnp.zeros((T, ROPE_LO), F32), -sin32[:, :ROPE_HALF],
                          jnp.zeros((T, ROPE_HALF + tail), F32)], 1)
    rb = jnp.concatenate([jnp.zeros((T, ROPE_LO + ROPE_HALF), F32), sin32[:, ROPE_HALF:],
                          jnp.zeros((T, tail), F32)], 1)

    x2 = x.reshape(T, D)
    for l in range(depth):
        cols = _split_w_in(w_in[l])
        win = _pack_w_in(cols)
        wzt = cols["zmla"].T.astype(BF16)
        wuq = _pad_heads(w_uq[l] * (MLA_QK ** -0.5 * LOG2E), MLA_QK).astype(BF16)
        wukv = w_ukv[l].reshape(MLA_KV_RANK, MLA_HEADS, MLA_NOPE + MLA_VDIM)
        wuk = _pad_heads(wukv[:, :, :MLA_NOPE].reshape(MLA_KV_RANK, -1), MLA_NOPE).astype(BF16)
        wuvt = wukv[:, :, MLA_NOPE:].reshape(MLA_KV_RANK, MLA_WIDTH).T.astype(BF16)
        wgg = jnp.pad(w_gla_gate[l], ((0, LANES - GLA_GATE_RANK), (0, 0))).astype(BF16)

        (q, k, vt, szmt, qg, kg, vg, la, szg, gm, gg) = _in_proj(
            x2, rc, ra, rb, g_in[l].reshape(1, D), win, wzt, g_q[l].reshape(1, -1), wuq,
            g_kv[l].reshape(1, -1), wuk, wuvt, wgg, b_gla_gate[l].reshape(1, -1), tm)

        omt = _mla_attn(q, k, vt, B, S, tq=256)
        o_gla = _gla(qg.reshape(B, S, GLA_DK), kg.reshape(B, S, GLA_DK),
                     vg.reshape(B, S, GLA_DV), la.reshape(B, S, GLA_DK),
                     g_gla[l].reshape(1, GLA_HV))

        x2 = _out_proj(x2, omt, szmt, o_gla.reshape(T, GLA_DV), szg,
                       gm, gg, w_proj_mla[l].astype(BF16), w_proj_gla[l].astype(BF16),
                       w_out[l].astype(BF16), g_final.reshape(1, D), tm,
                       final_norm=(l == depth - 1))
    return x2.reshape(B, S, D)
```

```python
import functools

import jax
import jax.numpy as jnp
from jax import lax
from jax.experimental import pallas as pl
from jax.experimental.pallas import tpu as pltpu

D_MODEL = 1024
EPS = 1e-6
MLA_HEADS = 8
MLA_NOPE = 64
MLA_ROPE = 32
MLA_VDIM = 64
MLA_Q_RANK = 384
MLA_KV_RANK = 256
MLA_QK = MLA_NOPE + MLA_ROPE
MLA_WIDTH = MLA_HEADS * MLA_VDIM
ROPE_THETA = 10000.0
GLA_HEADS = 4
GLA_DK = D_MODEL // 2
GLA_DV = D_MODEL
GLA_HK = GLA_DK // GLA_HEADS
GLA_HV = GLA_DV // GLA_HEADS
GLA_GATE_RANK = 16
GLA_GATE_NORM = 16.0
GLA_CHUNK = 64
SPLITS = (MLA_Q_RANK, MLA_KV_RANK, MLA_ROPE, MLA_WIDTH,
          GLA_DK, GLA_DK, GLA_DV, GLA_GATE_RANK, GLA_DV,
          D_MODEL, D_MODEL)

LANES = 128
SUBLANES = 8
HEAD_PAD = LANES
ROPE_HALF = MLA_ROPE // 2
ROPE_LO = MLA_NOPE
QPAD = MLA_HEADS * HEAD_PAD
LOG2E = 1.4426950408889634

_ROWS = {}
_o = 0
for _n, _w in zip(("cq", "ckv", "kr", "zmla", "qg", "kg", "vg", "alr", "zgla", "gmla", "ggla"),
                  SPLITS):
    _ROWS[_n] = (_o, _o + _w)
    _o += _w

VMEM_LIMIT = 52 * 1024 * 1024

BF16 = jnp.bfloat16
F32 = jnp.float32
_NT = (((1,), (1,)), ((), ()))
_TN = (((0,), (0,)), ((), ()))


def _const_spec(shape):
    nd = len(shape)
    return pl.BlockSpec(shape, lambda *_: (0,) * nd, pipeline_mode=pl.Buffered(1))


def _rms(v, g):
    return v * lax.rsqrt(jnp.mean(v * v, axis=-1, keepdims=True) + EPS) * g


def _in_proj_body(x_ref, pos_ref, freq_ref, gin_ref, wt_ref, gq_ref, wuqt_ref,
                  gkv_ref, wuk_ref, wuvt_ref, wgg_ref, bgg_ref,
                  qt_ref, k_ref, vt_ref, szmt_ref, qg_ref, kg_ref, vg_ref, la_ref,
                  szg_ref, gm_ref, gg_ref):
    h = _rms(x_ref[...], gin_ref[...]).astype(BF16)
    tm = h.shape[0]

    def tok(name):
        lo, hi = _ROWS[name]
        return lax.dot_general(h, wt_ref[lo:hi, :], _NT, preferred_element_type=F32)

    def feat(name):
        lo, hi = _ROWS[name]
        return lax.dot_general(wt_ref[lo:hi, :], h, _NT, preferred_element_type=F32)

    ang = freq_ref[...] * pos_ref[...]
    cos, sin = jnp.cos(ang), jnp.sin(ang)

    def rope(x1, x2):
        return x1 * cos - x2 * sin, x2 * cos + x1 * sin

    qn = _rms(tok("cq"), gq_ref[...]).astype(BF16)
    qt = lax.dot_general(wuqt_ref[...], qn, _NT, preferred_element_type=F32)
    for hd in range(MLA_HEADS):
        base = hd * HEAD_PAD
        lo = base + ROPE_LO
        r1, r2 = rope(qt[lo:lo + ROPE_HALF], qt[lo + ROPE_HALF:lo + MLA_ROPE])
        head = jnp.concatenate([qt[base:lo], r1, r2, qt[lo + MLA_ROPE:base + HEAD_PAD]], axis=0)
        qt_ref[base:base + HEAD_PAD, :] = head.astype(BF16)

    kvn = _rms(tok("ckv"), gkv_ref[...]).astype(BF16)
    kn = jnp.dot(kvn, wuk_ref[...], preferred_element_type=F32)
    krt = feat("kr")
    r1, r2 = rope(krt[:ROPE_HALF], krt[ROPE_HALF:])
    kr = jnp.concatenate([jnp.zeros((ROPE_LO, tm), F32), r1, r2,
                          jnp.zeros((HEAD_PAD - ROPE_LO - MLA_ROPE, tm), F32)], axis=0).T
    for hd in range(MLA_HEADS):
        sl = slice(hd * HEAD_PAD, (hd + 1) * HEAD_PAD)
        k_ref[:, sl] = (kn[:, sl] + kr).astype(BF16)
    vt_ref[...] = lax.dot_general(wuvt_ref[...], kvn, _NT,
                                  preferred_element_type=F32).astype(BF16)
    zt = feat("zmla")
    szmt_ref[...] = (zt * jax.nn.sigmoid(zt)).astype(BF16)

    qg_ref[...] = (tok("qg") * (GLA_HK ** -0.5)).astype(BF16)
    kg_ref[...] = tok("kg").astype(BF16)
    vg_ref[...] = tok("vg").astype(BF16)
    lo = _ROWS["alr"][0]
    alr = lax.dot_general(h, wt_ref[lo:lo + LANES, :], _NT, preferred_element_type=F32)
    lane = lax.broadcasted_iota(jnp.int32, alr.shape, 1)
    alr = jnp.where(lane < GLA_GATE_RANK, alr, 0.0).astype(BF16)
    zg = jnp.dot(alr, wgg_ref[...], preferred_element_type=F32) + bgg_ref[...]
    log_sig = jnp.minimum(zg, 0.0) - jnp.log1p(jnp.exp(-jnp.abs(zg)))
    la_ref[...] = log_sig * (1.0 / GLA_GATE_NORM)
    z = tok("zgla")
    szg_ref[...] = (z * jax.nn.sigmoid(z)).astype(BF16)

    gm_ref[...] = jax.nn.sigmoid(tok("gmla")).astype(BF16)
    gg_ref[...] = jax.nn.sigmoid(tok("ggla")).astype(BF16)


def _in_proj(x2, pos, freq, gin, wt, gq, wuqt, gkv, wuk, wuvt, wgg, bgg, tm):
    T = x2.shape[0]

    def row(w):
        return pl.BlockSpec((tm, w), lambda i: (i, 0))

    def colmajor(r):
        return pl.BlockSpec((r, tm), lambda i: (0, i))

    sds = jax.ShapeDtypeStruct
    out_shape = (sds((QPAD, T), BF16), sds((T, QPAD), BF16),
                 sds((MLA_WIDTH, T), BF16), sds((MLA_WIDTH, T), BF16),
                 sds((T, GLA_DK), BF16), sds((T, GLA_DK), BF16), sds((T, GLA_DV), BF16),
                 sds((T, GLA_DK), F32), sds((T, GLA_DV), BF16),
                 sds((T, D_MODEL), BF16), sds((T, D_MODEL), BF16))
    out_specs = (colmajor(QPAD), row(QPAD), colmajor(MLA_WIDTH), colmajor(MLA_WIDTH),
                 row(GLA_DK), row(GLA_DK), row(GLA_DV), row(GLA_DK), row(GLA_DV),
                 row(D_MODEL), row(D_MODEL))
    consts = (freq, gin, wt, gq, wuqt, gkv, wuk, wuvt, wgg, bgg)
    return pl.pallas_call(
        _in_proj_body,
        out_shape=out_shape,
        grid=(T // tm,),
        in_specs=[row(D_MODEL), colmajor(1)] + [_const_spec(c.shape) for c in consts],
        out_specs=out_specs,
        compiler_params=pltpu.CompilerParams(
            dimension_semantics=("arbitrary",), vmem_limit_bytes=VMEM_LIMIT),
        name="in_proj",
    )(x2, pos, *consts)


def _fold_rows(t, op):
    return op(t.reshape(t.shape[0] // SUBLANES, SUBLANES, t.shape[1]), axis=0)


def _mla_body(qt_ref, k_ref, vt_ref, o_ref, s_ref, p_ref, *, tq, nq):
    heads = 2
    row = lax.broadcasted_iota(jnp.int32, (tq, tq), 0)
    col = lax.broadcasted_iota(jnp.int32, (tq, tq), 1)
    keep = row <= col
    unit = 0
    for qi in range(nq):
        qrows = slice(qi * tq, (qi + 1) * tq)
        kv_len = (qi + 1) * tq
        for hh in range(heads):
            buf = unit % 2
            unit += 1
            lanes = slice(hh * HEAD_PAD, (hh + 1) * HEAD_PAD)
            qh = qt_ref[lanes, qrows]
            m8 = None
            for t in range(qi + 1):
                krows = slice(t * tq, (t + 1) * tq)
                s = jnp.dot(k_ref[krows, lanes], qh,
                            preferred_element_type=F32)
                if t == qi:
                    s = jnp.where(keep, s, -jnp.inf)
                s_ref[buf, krows, :] = s
                tile_max = _fold_rows(s, jnp.max)
                m8 = tile_max if m8 is None else jnp.maximum(m8, tile_max)
            m = jnp.max(m8, axis=0, keepdims=True)
            l8 = jnp.zeros((SUBLANES, tq), F32)
            for t in range(qi + 1):
                krows = slice(t * tq, (t + 1) * tq)
                p = jnp.exp2(s_ref[buf, krows, :] - m)
                l8 = l8 + _fold_rows(p, jnp.sum)
                p_ref[buf, krows, :] = p.astype(BF16)
            l = jnp.sum(l8, axis=0, keepdims=True)
            vrows = slice(hh * MLA_VDIM, (hh + 1) * MLA_VDIM)
            o_t = jnp.dot(vt_ref[vrows, 0:kv_len], p_ref[buf, 0:kv_len, :],
                          preferred_element_type=F32)
            o_ref[vrows, qrows] = (o_t / l).astype(BF16)


def _mla_attn(qt, k2, vt, B, S, tq):
    pairs = MLA_HEADS // 2
    return pl.pallas_call(
        functools.partial(_mla_body, tq=tq, nq=S // tq),
        out_shape=jax.ShapeDtypeStruct((MLA_WIDTH, B * S), BF16),
        grid=(B, pairs),
        in_specs=[pl.BlockSpec((2 * HEAD_PAD, S), lambda b, p: (p, b)),
                  pl.BlockSpec((S, 2 * HEAD_PAD), lambda b, p: (b, p)),
                  pl.BlockSpec((2 * MLA_VDIM, S), lambda b, p: (p, b))],
        out_specs=pl.BlockSpec((2 * MLA_VDIM, S), lambda b, p: (p, b)),
        scratch_shapes=[pltpu.VMEM((2, S, tq), F32), pltpu.VMEM((2, S, tq), BF16)],
        compiler_params=pltpu.CompilerParams(
            dimension_semantics=("arbitrary", "arbitrary"), vmem_limit_bytes=VMEM_LIMIT),
        name="mla_attn",
    )(qt, k2, vt)


GLA_BLOCK = 256


def _gla_body(q_ref, k_ref, v_ref, la_ref, g_ref, o_ref, oi_ref, qin_ref, u_ref, dec_ref,
              *, seq):
    C, BLK = GLA_CHUNK, GLA_BLOCK
    per_blk = BLK // C
    r = lax.broadcasted_iota(jnp.int32, (BLK, BLK), 0)
    c = lax.broadcasted_iota(jnp.int32, (BLK, BLK), 1)
    causal = (c <= r) & (r // C == c // C)
    tril = jnp.where(causal, 1.0, 0.0).astype(BF16)
    gain = g_ref[...]

    for blk in range(seq // BLK):
        rows = slice(blk * BLK, (blk + 1) * BLK)
        g = la_ref[rows, :]
        g_hi = g.astype(BF16)
        g_lo = (g - g_hi.astype(F32)).astype(BF16)
        bb = jnp.dot(tril, jnp.concatenate([g_hi, g_lo], axis=1), preferred_element_type=F32)
        b = bb[:, :GLA_HK] + bb[:, GLA_HK:]
        lasts = [b[(j + 1) * C - 1:(j + 1) * C, :] for j in range(per_blk)]
        b_last = jnp.concatenate([jnp.broadcast_to(t, (C, GLA_HK)) for t in lasts], axis=0)
        q = q_ref[rows, :].astype(F32)
        k = k_ref[rows, :].astype(F32)
        q_in = (q * jnp.exp(b)).astype(BF16)
        k_in = (k * jnp.exp(-b)).astype(BF16)
        k_st = (k * jnp.exp(b_last - b)).astype(BF16)
        attn = lax.dot_general(q_in, k_in, _NT, preferred_element_type=F32)
        attn = jnp.where(causal, attn, 0.0).astype(BF16)
        oi_ref[rows, :] = jnp.dot(attn, v_ref[rows, :], preferred_element_type=F32)
        qin_ref[rows, :] = q_in
        for j in range(per_blk):
            n = blk * per_blk + j
            crow = slice(n * C, (n + 1) * C)
            u_ref[n] = lax.dot_general(v_ref[crow, :], k_st[j * C:(j + 1) * C, :], _TN,
                                       preferred_element_type=F32)
            dec_ref[n] = jnp.exp(lasts[j])

    st = jnp.zeros((GLA_HV, GLA_HK), F32)
    for n in range(seq // C):
        crow = slice(n * C, (n + 1) * C)
        o = oi_ref[crow, :] + lax.dot_general(qin_ref[crow, :], st.astype(BF16), _NT,
                                              preferred_element_type=F32)
        o_ref[crow, :] = _rms(o, gain).astype(BF16)
        st = st * dec_ref[n] + u_ref[n]


def _gla(q3, k3, v3, la3, gain):
    B, S, _ = q3.shape
    n_chunks = S // GLA_CHUNK

    def spec(w):
        return pl.BlockSpec((None, S, w), lambda b, h: (b, 0, h))

    return pl.pallas_call(
        functools.partial(_gla_body, seq=S),
        out_shape=jax.ShapeDtypeStruct((B, S, GLA_DV), BF16),
        grid=(B, GLA_HEADS),
        in_specs=[spec(GLA_HK), spec(GLA_HK), spec(GLA_HV), spec(GLA_HK),
                  _const_spec(gain.shape)],
        out_specs=spec(GLA_HV),
        scratch_shapes=[pltpu.VMEM((S, GLA_HV), F32), pltpu.VMEM((S, GLA_HK), BF16),
                        pltpu.VMEM((n_chunks, GLA_HV, GLA_HK), F32),
                        pltpu.VMEM((n_chunks, 1, GLA_HK), F32)],
        compiler_params=pltpu.CompilerParams(
            dimension_semantics=("arbitrary", "arbitrary"), vmem_limit_bytes=VMEM_LIMIT),
        name="gla",
    )(q3, k3, v3, la3, gain)


def _out_body(x_ref, omt_ref, szmt_ref, og_ref, szg_ref, gm_ref, gg_ref,
              wpm_ref, wpg_ref, wo_ref, gf_ref, o_ref, *, final_norm):
    umt = (omt_ref[...].astype(F32) * szmt_ref[...].astype(F32)).astype(BF16)
    ug = (og_ref[...].astype(F32) * szg_ref[...].astype(F32)).astype(BF16)
    y_mla = lax.dot_general(umt, wpm_ref[...], _TN, preferred_element_type=F32)
    y_gla = jnp.dot(ug, wpg_ref[...], preferred_element_type=F32)
    merged = gm_ref[...].astype(F32) * y_mla + gg_ref[...].astype(F32) * y_gla
    r = x_ref[...] + jnp.dot(merged.astype(BF16), wo_ref[...], preferred_element_type=F32)
    o_ref[...] = _rms(r, gf_ref[...]) if final_norm else r


def _out_proj(x2, omt, szmt, og, szg, gm, gg, wpm, wpg, wo, gf, tm, final_norm):
    T = x2.shape[0]

    def row(w):
        return pl.BlockSpec((tm, w), lambda i: (i, 0))

    def colmajor(r):
        return pl.BlockSpec((r, tm), lambda i: (0, i))

    return pl.pallas_call(
        functools.partial(_out_body, final_norm=final_norm),
        out_shape=jax.ShapeDtypeStruct((T, D_MODEL), F32),
        grid=(T // tm,),
        in_specs=[row(D_MODEL), colmajor(MLA_WIDTH), colmajor(MLA_WIDTH), row(GLA_DV),
                  row(GLA_DV), row(D_MODEL), row(D_MODEL), _const_spec(wpm.shape),
                  _const_spec(wpg.shape), _const_spec(wo.shape), _const_spec(gf.shape)],
        out_specs=row(D_MODEL),
        compiler_params=pltpu.CompilerParams(
            dimension_semantics=("arbitrary",), vmem_limit_bytes=VMEM_LIMIT),
        name="out_proj",
    )(x2, omt, szmt, og, szg, gm, gg, wpm, wpg, wo, gf)


def _pad_heads(w, width):
    r = w.shape[0]
    w = w.reshape(r, MLA_HEADS, width)
    w = jnp.pad(w, ((0, 0), (0, 0), (0, HEAD_PAD - width)))
    return w.reshape(r, QPAD)


def kernel(x, positions, g_in, w_in, g_q, w_uq, g_kv, w_ukv, w_gla_gate, b_gla_gate,
           g_gla, w_proj_mla, w_proj_gla, w_out, g_final):
    B, S, D = x.shape
    T = B * S
    depth = w_in.shape[0]
    tm = 512

    freq = (ROPE_THETA ** (-jnp.arange(ROPE_HALF, dtype=F32) / ROPE_HALF)).reshape(ROPE_HALF, 1)
    pos = positions.astype(F32).reshape(1, T)

    x2 = x.reshape(T, D)
    for l in range(depth):
        wt = w_in[l].T.astype(BF16)
        wuqt = _pad_heads(w_uq[l] * (MLA_QK ** -0.5 * LOG2E), MLA_QK).T.astype(BF16)
        wukv = w_ukv[l].reshape(MLA_KV_RANK, MLA_HEADS, MLA_NOPE + MLA_VDIM)
        wuk = _pad_heads(wukv[:, :, :MLA_NOPE].reshape(MLA_KV_RANK, -1), MLA_NOPE).astype(BF16)
        wuvt = wukv[:, :, MLA_NOPE:].reshape(MLA_KV_RANK, MLA_WIDTH).T.astype(BF16)
        wgg = jnp.pad(w_gla_gate[l], ((0, LANES - GLA_GATE_RANK), (0, 0))).astype(BF16)

        (qt, k, vt, szmt, qg, kg, vg, la, szg, gm, gg) = _in_proj(
            x2, pos, freq, g_in[l].reshape(1, D), wt, g_q[l].reshape(1, -1), wuqt,
            g_kv[l].reshape(1, -1), wuk, wuvt, wgg, b_gla_gate[l].reshape(1, -1), tm)

        omt = _mla_attn(qt, k, vt, B, S, tq=256)
        o_gla = _gla(qg.reshape(B, S, GLA_DK), kg.reshape(B, S, GLA_DK),
                     vg.reshape(B, S, GLA_DV), la.reshape(B, S, GLA_DK),
                     g_gla[l].reshape(1, GLA_HV))

        x2 = _out_proj(x2, omt, szmt, o_gla.reshape(T, GLA_DV), szg,
                       gm, gg, w_proj_mla[l].astype(BF16), w_proj_gla[l].astype(BF16),
                       w_out[l].astype(BF16), g_final.reshape(1, D), tm,
                       final_norm=(l == depth - 1))
    return x2.reshape(B, S, D)
```

```python
import functools

import jax
import jax.numpy as jnp
from jax import lax
from jax.experimental import pallas as pl
from jax.experimental.pallas import tpu as pltpu

D_MODEL = 1024
EPS = 1e-6
MLA_HEADS = 8
MLA_NOPE = 64
MLA_ROPE = 32
MLA_VDIM = 64
MLA_Q_RANK = 384
MLA_KV_RANK = 256
MLA_QK = MLA_NOPE + MLA_ROPE
MLA_WIDTH = MLA_HEADS * MLA_VDIM
ROPE_THETA = 10000.0
GLA_HEADS = 4
GLA_DK = D_MODEL // 2
GLA_DV = D_MODEL
GLA_HK = GLA_DK // GLA_HEADS
GLA_HV = GLA_DV // GLA_HEADS
GLA_GATE_RANK = 16
GLA_GATE_NORM = 16.0
GLA_CHUNK = 64
SPLITS = (MLA_Q_RANK, MLA_KV_RANK, MLA_ROPE, MLA_WIDTH,
          GLA_DK, GLA_DK, GLA_DV, GLA_GATE_RANK, GLA_DV,
          D_MODEL, D_MODEL)

LANES = 128
SUBLANES = 8
HEAD_PAD = LANES
ROPE_HALF = MLA_ROPE // 2
ROPE_LO = MLA_NOPE
QPAD = MLA_HEADS * HEAD_PAD
LOG2E = 1.4426950408889634

_ROWS = {}
_o = 0
for _n, _w in zip(("cq", "ckv", "kr", "zmla", "qg", "kg", "vg", "alr", "zgla", "gmla", "ggla"),
                  SPLITS):
    _ROWS[_n] = (_o, _o + _w)
    _o += _w

VMEM_LIMIT = 52 * 1024 * 1024

BF16 = jnp.bfloat16
F32 = jnp.float32
_NT = (((1,), (1,)), ((), ()))
_TN = (((0,), (0,)), ((), ()))


def _const_spec(shape):
    nd = len(shape)
    return pl.BlockSpec(shape, lambda *_: (0,) * nd, pipeline_mode=pl.Buffered(1))


def _rms(v, g):
    return v * lax.rsqrt(jnp.mean(v * v, axis=-1, keepdims=True) + EPS) * g


def _in_proj_body(x_ref, pos_ref, freq_ref, gin_ref, wt_ref, gq_ref, wuqt_ref,
                  gkv_ref, wuk_ref, wuvt_ref, wgg_ref, bgg_ref,
                  qt_ref, k_ref, vt_ref, szmt_ref, qg_ref, kg_ref, vg_ref, la_ref,
                  szg_ref, gm_ref, gg_ref):
    h = _rms(x_ref[...], gin_ref[...]).astype(BF16)
    tm = h.shape[0]

    def tok(name):
        lo, hi = _ROWS[name]
        return lax.dot_general(h, wt_ref[lo:hi, :], _NT, preferred_element_type=F32)

    def feat(name):
        lo, hi = _ROWS[name]
        return lax.dot_general(wt_ref[lo:hi, :], h, _NT, preferred_element_type=F32)

    ang = freq_ref[...] * pos_ref[...]
    cos, sin = jnp.cos(ang), jnp.sin(ang)

    def rope(x1, x2):
        return x1 * cos - x2 * sin, x2 * cos + x1 * sin

    qn = _rms(tok("cq"), gq_ref[...]).astype(BF16)
    qt = lax.dot_general(wuqt_ref[...], qn, _NT, preferred_element_type=F32)
    for hd in range(MLA_HEADS):
        base = hd * HEAD_PAD
        lo = base + ROPE_LO
        r1, r2 = rope(qt[lo:lo + ROPE_HALF], qt[lo + ROPE_HALF:lo + MLA_ROPE])
        head = jnp.concatenate([qt[base:lo], r1, r2, qt[lo + MLA_ROPE:base + HEAD_PAD]], axis=0)
        qt_ref[base:base + HEAD_PAD, :] = head.astype(BF16)

    kvn = _rms(tok("ckv"), gkv_ref[...]).astype(BF16)
    kn = jnp.dot(kvn, wuk_ref[...], preferred_element_type=F32)
    krt = feat("kr")
    r1, r2 = rope(krt[:ROPE_HALF], krt[ROPE_HALF:])
    kr = jnp.concatenate([jnp.zeros((ROPE_LO, tm), F32), r1, r2,
                          jnp.zeros((HEAD_PAD - ROPE_LO - MLA_ROPE, tm), F32)], axis=0).T
    for hd in range(MLA_HEADS):
        sl = slice(hd * HEAD_PAD, (hd + 1) * HEAD_PAD)
        k_ref[:, sl] = (kn[:, sl] + kr).astype(BF16)
    vt_ref[...] = lax.dot_general(wuvt_ref[...], kvn, _NT,
                                  preferred_element_type=F32).astype(BF16)
    zt = feat("zmla")
    szmt_ref[...] = (zt * jax.nn.sigmoid(zt)).astype(BF16)

    qg_ref[...] = (tok("qg") * (GLA_HK ** -0.5)).astype(BF16)
    kg_ref[...] = tok("kg").astype(BF16)
    vg_ref[...] = tok("vg").astype(BF16)
    lo = _ROWS["alr"][0]
    alr = lax.dot_general(h, wt_ref[lo:lo + LANES, :], _NT, preferred_element_type=F32)
    lane = lax.broadcasted_iota(jnp.int32, alr.shape, 1)
    alr = jnp.where(lane < GLA_GATE_RANK, alr, 0.0).astype(BF16)
    zg = jnp.dot(alr, wgg_ref[...], preferred_element_type=F32) + bgg_ref[...]
    log_sig = jnp.minimum(zg, 0.0) - jnp.log1p(jnp.exp(-jnp.abs(zg)))
    la_ref[...] = log_sig * (1.0 / GLA_GATE_NORM)
    z = tok("zgla")
    szg_ref[...] = (z * jax.nn.sigmoid(z)).astype(BF16)

    gm_ref[...] = jax.nn.sigmoid(tok("gmla")).astype(BF16)
    gg_ref[...] = jax.nn.sigmoid(tok("ggla")).astype(BF16)


def _in_proj(x2, pos, freq, gin, wt, gq, wuqt, gkv, wuk, wuvt, wgg, bgg, tm):
    T = x2.shape[0]

    def row(w):
        return pl.BlockSpec((tm, w), lambda i: (i, 0))

    def colmajor(r):
        return pl.BlockSpec((r, tm), lambda i: (0, i))

    sds = jax.ShapeDtypeStruct
    out_shape = (sds((QPAD, T), BF16), sds((T, QPAD), BF16),
                 sds((MLA_WIDTH, T), BF16), sds((MLA_WIDTH, T), BF16),
                 sds((T, GLA_DK), BF16), sds((T, GLA_DK), BF16), sds((T, GLA_DV), BF16),
                 sds((T, GLA_DK), F32), sds((T, GLA_DV), BF16),
                 sds((T, D_MODEL), BF16), sds((T, D_MODEL), BF16))
    out_specs = (colmajor(QPAD), row(QPAD), colmajor(MLA_WIDTH), colmajor(MLA_WIDTH),
                 row(GLA_DK), row(GLA_DK), row(GLA_DV), row(GLA_DK), row(GLA_DV),
                 row(D_MODEL), row(D_MODEL))
    consts = (freq, gin, wt, gq, wuqt, gkv, wuk, wuvt, wgg, bgg)
    return pl.pallas_call(
        _in_proj_body,
        out_shape=out_shape,
        grid=(T // tm,),
        in_specs=[row(D_MODEL), colmajor(1)] + [_const_spec(c.shape) for c in consts],
        out_specs=out_specs,
        compiler_params=pltpu.CompilerParams(
            dimension_semantics=("arbitrary",), vmem_limit_bytes=VMEM_LIMIT),
        name="in_proj",
    )(x2, pos, *consts)


MLA_BUFS = 4


def _fold_rows(t, op):
    return op(t.reshape(t.shape[0] // SUBLANES, SUBLANES, t.shape[1]), axis=0)


def _mla_body(qt_ref, k_ref, vt_ref, o_ref, s_ref, p_ref, *, tq, nq):
    heads = 2
    row = lax.broadcasted_iota(jnp.int32, (tq, tq), 0)
    col = lax.broadcasted_iota(jnp.int32, (tq, tq), 1)
    keep = row <= col
    units = [(qi, hh) for qi in range(nq) for hh in range(heads)]
    stats = [dict() for _ in units]

    def tiles(qi):
        return [slice(t * tq, (t + 1) * tq) for t in range(qi + 1)]

    def scores(u):
        qi, hh = units[u]
        lanes = slice(hh * HEAD_PAD, (hh + 1) * HEAD_PAD)
        qh = qt_ref[lanes, qi * tq:(qi + 1) * tq]
        m8 = None
        for t, krows in enumerate(tiles(qi)):
            s = jnp.dot(k_ref[krows, lanes], qh, preferred_element_type=F32)
            if t == qi:
                s = jnp.where(keep, s, -jnp.inf)
            s_ref[u % MLA_BUFS, krows, :] = s
            tile_max = _fold_rows(s, jnp.max)
            m8 = tile_max if m8 is None else jnp.maximum(m8, tile_max)
            yield
        stats[u]["m"] = jnp.max(m8, axis=0, keepdims=True)

    def probs(u):
        qi, _ = units[u]
        m = stats[u]["m"]
        l8 = jnp.zeros((SUBLANES, tq), F32)
        for krows in tiles(qi):
            p = jnp.exp2(s_ref[u % MLA_BUFS, krows, :] - m)
            l8 = l8 + _fold_rows(p, jnp.sum)
            p_ref[u % MLA_BUFS, krows, :] = p.astype(BF16)
            yield
        stats[u]["l"] = jnp.sum(l8, axis=0, keepdims=True)

    def values(u):
        qi, hh = units[u]
        kv_len = (qi + 1) * tq
        vrows = slice(hh * MLA_VDIM, (hh + 1) * MLA_VDIM)
        o_t = jnp.dot(vt_ref[vrows, 0:kv_len], p_ref[u % MLA_BUFS, 0:kv_len, :],
                      preferred_element_type=F32)
        o_ref[vrows, qi * tq:(qi + 1) * tq] = (o_t / stats[u]["l"]).astype(BF16)
        yield

    stages = (scores, probs, values)
    for step in range(len(units) + len(stages) - 1):
        live = [stage(step - d) for d, stage in enumerate(stages)
                if 0 <= step - d < len(units)]
        while live:
            live = [g for g in live if next(g, StopIteration) is not StopIteration]


def _mla_attn(qt, k2, vt, B, S, tq):
    pairs = MLA_HEADS // 2
    return pl.pallas_call(
        functools.partial(_mla_body, tq=tq, nq=S // tq),
        out_shape=jax.ShapeDtypeStruct((MLA_WIDTH, B * S), BF16),
        grid=(B, pairs),
        in_specs=[pl.BlockSpec((2 * HEAD_PAD, S), lambda b, p: (p, b)),
                  pl.BlockSpec((S, 2 * HEAD_PAD), lambda b, p: (b, p)),
                  pl.BlockSpec((2 * MLA_VDIM, S), lambda b, p: (p, b))],
        out_specs=pl.BlockSpec((2 * MLA_VDIM, S), lambda b, p: (p, b)),
        scratch_shapes=[pltpu.VMEM((MLA_BUFS, S, tq), F32),
                        pltpu.VMEM((MLA_BUFS, S, tq), BF16)],
        compiler_params=pltpu.CompilerParams(
            dimension_semantics=("arbitrary", "arbitrary"), vmem_limit_bytes=VMEM_LIMIT),
        name="mla_attn",
    )(qt, k2, vt)


GLA_BLOCK = 256


def _gla_body(q_ref, k_ref, v_ref, la_ref, g_ref, o_ref, oi_ref, qin_ref, u_ref, dec_ref,
              *, seq):
    C, BLK = GLA_CHUNK, GLA_BLOCK
    per_blk = BLK // C
    r = lax.broadcasted_iota(jnp.int32, (BLK, BLK), 0)
    c = lax.broadcasted_iota(jnp.int32, (BLK, BLK), 1)
    causal = (c <= r) & (r // C == c // C)
    tril = jnp.where(causal, 1.0, 0.0).astype(BF16)
    gain = g_ref[...]

    chunk_of_row = lax.broadcasted_iota(jnp.int32, (BLK, GLA_HK), 0) // C
    n_blk = seq // BLK
    vals = [dict() for _ in range(n_blk)]
    carry = {"st": jnp.zeros((GLA_HV, GLA_HK), F32)}

    def decay(blk):
        g = la_ref[blk * BLK:(blk + 1) * BLK, :]
        g_hi = g.astype(BF16)
        g_lo = (g - g_hi.astype(F32)).astype(BF16)
        bb = jnp.dot(tril, jnp.concatenate([g_hi, g_lo], axis=1), preferred_element_type=F32)
        vals[blk]["b"] = bb[:, :GLA_HK] + bb[:, GLA_HK:]
        yield

    def scale(blk):
        rows = slice(blk * BLK, (blk + 1) * BLK)
        b = vals[blk].pop("b")
        lasts = [b[(j + 1) * C - 1:(j + 1) * C, :] for j in range(per_blk)]
        b_last = jnp.concatenate([jnp.broadcast_to(t, (C, GLA_HK)) for t in lasts], axis=0)
        q = q_ref[rows, :].astype(F32)
        k = k_ref[rows, :].astype(F32)
        q_in = (q * jnp.exp(b)).astype(BF16)
        k_in = (k * jnp.exp(-b)).astype(BF16)
        k_st = (k * jnp.exp(b_last - b)).astype(BF16)
        yield
        qin_ref[rows, :] = q_in
        for j in range(per_blk):
            dec_ref[blk * per_blk + j] = jnp.exp(lasts[j])
        zero = jnp.zeros_like(k_st)
        vals[blk]["k_bd"] = jnp.concatenate(
            [jnp.where(chunk_of_row == j, k_st, zero) for j in range(per_blk)], axis=1)
        vals[blk]["attn"] = lax.dot_general(q_in, k_in, _NT, preferred_element_type=F32)
        yield

    def mix(blk):
        rows = slice(blk * BLK, (blk + 1) * BLK)
        attn = jnp.where(causal, vals[blk].pop("attn"), 0.0).astype(BF16)
        v = v_ref[rows, :]
        oi_ref[rows, :] = jnp.dot(attn, v, preferred_element_type=F32)
        yield
        u_all = lax.dot_general(v, vals[blk].pop("k_bd"), _TN,
                                preferred_element_type=F32)
        for j in range(per_blk):
            u_ref[blk * per_blk + j] = u_all[:, j * GLA_HK:(j + 1) * GLA_HK]
        yield

    def scan(blk):
        for n in range(blk * per_blk, (blk + 1) * per_blk):
            crow = slice(n * C, (n + 1) * C)
            st = carry["st"]
            o = oi_ref[crow, :] + lax.dot_general(qin_ref[crow, :], st.astype(BF16), _NT,
                                                  preferred_element_type=F32)
            o_ref[crow, :] = _rms(o, gain).astype(BF16)
            carry["st"] = st * dec_ref[n] + u_ref[n]
            yield

    stages = (decay, scale, mix, scan)
    for step in range(n_blk + len(stages) - 1):
        live = [stage(step - d) for d, stage in enumerate(stages) if 0 <= step - d < n_blk]
        while live:
            live = [g for g in live if next(g, StopIteration) is not StopIteration]


def _gla(q3, k3, v3, la3, gain):
    B, S, _ = q3.shape
    n_chunks = S // GLA_CHUNK

    def spec(w):
        return pl.BlockSpec((None, S, w), lambda b, h: (b, 0, h))

    return pl.pallas_call(
        functools.partial(_gla_body, seq=S),
        out_shape=jax.ShapeDtypeStruct((B, S, GLA_DV), BF16),
        grid=(B, GLA_HEADS),
        in_specs=[spec(GLA_HK), spec(GLA_HK), spec(GLA_HV), spec(GLA_HK),
                  _const_spec(gain.shape)],
        out_specs=spec(GLA_HV),
        scratch_shapes=[pltpu.VMEM((S, GLA_HV), F32), pltpu.VMEM((S, GLA_HK), BF16),
                        pltpu.VMEM((n_chunks, GLA_HV, GLA_HK), F32),
                        pltpu.VMEM((n_chunks, 1, GLA_HK), F32)],
        compiler_params=pltpu.CompilerParams(
            dimension_semantics=("arbitrary", "arbitrary"), vmem_limit_bytes=VMEM_LIMIT),
        name="gla",
    )(q3, k3, v3, la3, gain)


def _out_body(x_ref, omt_ref, szmt_ref, og_ref, szg_ref, gm_ref, gg_ref,
              wpm_ref, wpg_ref, wo_ref, gf_ref, o_ref, *, final_norm):
    umt = (omt_ref[...].astype(F32) * szmt_ref[...].astype(F32)).astype(BF16)
    ug = (og_ref[...].astype(F32) * szg_ref[...].astype(F32)).astype(BF16)
    y_mla = lax.dot_general(umt, wpm_ref[...], _TN, preferred_element_type=F32)
    y_gla = jnp.dot(ug, wpg_ref[...], preferred_element_type=F32)
    merged = gm_ref[...].astype(F32) * y_mla + gg_ref[...].astype(F32) * y_gla
    r = x_ref[...] + jnp.dot(merged.astype(BF16), wo_ref[...], preferred_element_type=F32)
    o_ref[...] = _rms(r, gf_ref[...]) if final_norm else r


def _out_proj(x2, omt, szmt, og, szg, gm, gg, wpm, wpg, wo, gf, tm, final_norm):
    T = x2.shape[0]

    def row(w):
        return pl.BlockSpec((tm, w), lambda i: (i, 0))

    def colmajor(r):
        return pl.BlockSpec((r, tm), lambda i: (0, i))

    return pl.pallas_call(
        functools.partial(_out_body, final_norm=final_norm),
        out_shape=jax.ShapeDtypeStruct((T, D_MODEL), F32),
        grid=(T // tm,),
        in_specs=[row(D_MODEL), colmajor(MLA_WIDTH), colmajor(MLA_WIDTH), row(GLA_DV),
                  row(GLA_DV), row(D_MODEL), row(D_MODEL), _const_spec(wpm.shape),
                  _const_spec(wpg.shape), _const_spec(wo.shape), _const_spec(gf.shape)],
        out_specs=row(D_MODEL),
        compiler_params=pltpu.CompilerParams(
            dimension_semantics=("arbitrary",), vmem_limit_bytes=VMEM_LIMIT),
        name="out_proj",
    )(x2, omt, szmt, og, szg, gm, gg, wpm, wpg, wo, gf)


def _pad_heads(w, width):
    r = w.shape[0]
    w = w.reshape(r, MLA_HEADS, width)
    w = jnp.pad(w, ((0, 0), (0, 0), (0, HEAD_PAD - width)))
    return w.reshape(r, QPAD)


def kernel(x, positions, g_in, w_in, g_q, w_uq, g_kv, w_ukv, w_gla_gate, b_gla_gate,
           g_gla, w_proj_mla, w_proj_gla, w_out, g_final):
    B, S, D = x.shape
    T = B * S
    depth = w_in.shape[0]
    tm = 512

    freq = (ROPE_THETA ** (-jnp.arange(ROPE_HALF, dtype=F32) / ROPE_HALF)).reshape(ROPE_HALF, 1)
    pos = positions.astype(F32).reshape(1, T)

    x2 = x.reshape(T, D)
    for l in range(depth):
        wt = w_in[l].T.astype(BF16)
        wuqt = _pad_heads(w_uq[l] * (MLA_QK ** -0.5 * LOG2E), MLA_QK).T.astype(BF16)
        wukv = w_ukv[l].reshape(MLA_KV_RANK, MLA_HEADS, MLA_NOPE + MLA_VDIM)
        wuk = _pad_heads(wukv[:, :, :MLA_NOPE].reshape(MLA_KV_RANK, -1), MLA_NOPE).astype(BF16)
        wuvt = wukv[:, :, MLA_NOPE:].reshape(MLA_KV_RANK, MLA_WIDTH).T.astype(BF16)
        wgg = jnp.pad(w_gla_gate[l], ((0, LANES - GLA_GATE_RANK), (0, 0))).astype(BF16)

        (qt, k, vt, szmt, qg, kg, vg, la, szg, gm, gg) = _in_proj(
            x2, pos, freq, g_in[l].reshape(1, D), wt, g_q[l].reshape(1, -1), wuqt,
            g_kv[l].reshape(1, -1), wuk, wuvt, wgg, b_gla_gate[l].reshape(1, -1), tm)

        omt = _mla_attn(qt, k, vt, B, S, tq=256)
        o_gla = _gla(qg.reshape(B, S, GLA_DK), kg.reshape(B, S, GLA_DK),
                     vg.reshape(B, S, GLA_DV), la.reshape(B, S, GLA_DK),
                     g_gla[l].reshape(1, GLA_HV))

        x2 = _out_proj(x2, omt, szmt, o_gla.reshape(T, GLA_DV), szg,
                       gm, gg, w_proj_mla[l].astype(BF16), w_proj_gla[l].astype(BF16),
                       w_out[l].astype(BF16), g_final.reshape(1, D), tm,
                       final_norm=(l == depth - 1))
    return x2.reshape(B, S, D)
```

```python
import functools

import jax
import jax.numpy as jnp
from jax import lax
from jax.experimental import pallas as pl
from jax.experimental.pallas import tpu as pltpu

D_MODEL = 1024
EPS = 1e-6
MLA_HEADS = 8
MLA_NOPE = 64
MLA_ROPE = 32
MLA_VDIM = 64
MLA_Q_RANK = 384
MLA_KV_RANK = 256
MLA_QK = MLA_NOPE + MLA_ROPE
MLA_WIDTH = MLA_HEADS * MLA_VDIM
ROPE_THETA = 10000.0
GLA_HEADS = 4
GLA_DK = D_MODEL // 2
GLA_DV = D_MODEL
GLA_HK = GLA_DK // GLA_HEADS
GLA_HV = GLA_DV // GLA_HEADS
GLA_GATE_RANK = 16
GLA_GATE_NORM = 16.0
GLA_CHUNK = 64
SPLITS = (MLA_Q_RANK, MLA_KV_RANK, MLA_ROPE, MLA_WIDTH,
          GLA_DK, GLA_DK, GLA_DV, GLA_GATE_RANK, GLA_DV,
          D_MODEL, D_MODEL)

LANES = 128
SUBLANES = 8
HEAD_PAD = LANES
ROPE_HALF = MLA_ROPE // 2
ROPE_LO = MLA_NOPE
QPAD = MLA_HEADS * HEAD_PAD
LOG2E = 1.4426950408889634

_ROWS = {}
_o = 0
for _n, _w in zip(("cq", "ckv", "kr", "zmla", "qg", "kg", "vg", "alr", "zgla", "gmla", "ggla"),
                  SPLITS):
    _ROWS[_n] = (_o, _o + _w)
    _o += _w

VMEM_LIMIT = 52 * 1024 * 1024

BF16 = jnp.bfloat16
F32 = jnp.float32
_NT = (((1,), (1,)), ((), ()))
_TN = (((0,), (0,)), ((), ()))


def _const_spec(shape):
    nd = len(shape)
    return pl.BlockSpec(shape, lambda *_: (0,) * nd, pipeline_mode=pl.Buffered(1))


def _rms(v, g):
    return v * lax.rsqrt(jnp.mean(v * v, axis=-1, keepdims=True) + EPS) * g


def _in_proj_body(x_ref, pos_ref, freq_ref, gin_ref, wt_ref, gq_ref, wuqt_ref,
                  gkv_ref, wuk_ref, wuvt_ref, wgg_ref, bgg_ref,
                  qt_ref, k_ref, vt_ref, szmt_ref, qg_ref, kg_ref, vg_ref, la_ref,
                  szg_ref, gm_ref, gg_ref):
    h = _rms(x_ref[...], gin_ref[...]).astype(BF16)
    tm = h.shape[0]

    def tok(name):
        lo, hi = _ROWS[name]
        return lax.dot_general(h, wt_ref[lo:hi, :], _NT, preferred_element_type=F32)

    def sigmoid_to(ref, z, gate=False):
        sg = jax.nn.sigmoid(z)
        ref[...] = (z * sg if gate else sg).astype(BF16)

    cq = tok("cq")
    ckv = tok("ckv")
    qn = _rms(cq, gq_ref[...]).astype(BF16)
    lo = _ROWS["kr"][0]
    krz = lax.dot_general(wt_ref[lo:_ROWS["zmla"][1], :], h, _NT,
                          preferred_element_type=F32)
    kvn = _rms(ckv, gkv_ref[...]).astype(BF16)
    qt = lax.dot_general(wuqt_ref[...], qn, _NT, preferred_element_type=F32)

    ang = freq_ref[...] * pos_ref[...]
    cos, sin = jnp.cos(ang), jnp.sin(ang)

    def rope(x1, x2):
        return x1 * cos - x2 * sin, x2 * cos + x1 * sin

    gm = tok("gmla")
    for hd in range(MLA_HEADS):
        base = hd * HEAD_PAD
        lo = base + ROPE_LO
        r1, r2 = rope(qt[lo:lo + ROPE_HALF], qt[lo + ROPE_HALF:lo + MLA_ROPE])
        head = jnp.concatenate([qt[base:lo], r1, r2, qt[lo + MLA_ROPE:base + HEAD_PAD]], axis=0)
        qt_ref[base:base + HEAD_PAD, :] = head.astype(BF16)

    kn = jnp.dot(kvn, wuk_ref[...], preferred_element_type=F32)
    sigmoid_to(gm_ref, gm)
    gg = tok("ggla")
    r1, r2 = rope(krz[:ROPE_HALF], krz[ROPE_HALF:MLA_ROPE])
    kr = jnp.concatenate([jnp.zeros((ROPE_LO, tm), F32), r1, r2,
                          jnp.zeros((HEAD_PAD - ROPE_LO - MLA_ROPE, tm), F32)], axis=0).T
    for hd in range(MLA_HEADS):
        sl = slice(hd * HEAD_PAD, (hd + 1) * HEAD_PAD)
        k_ref[:, sl] = (kn[:, sl] + kr).astype(BF16)

    vt = lax.dot_general(wuvt_ref[...], kvn, _NT, preferred_element_type=F32)
    sigmoid_to(gg_ref, gg)
    zg_full = tok("zgla")
    vt_ref[...] = vt.astype(BF16)
    sigmoid_to(szmt_ref, krz[MLA_ROPE:], gate=True)
    qg = tok("qg")
    sigmoid_to(szg_ref, zg_full, gate=True)
    kg = tok("kg")
    qg_ref[...] = (qg * (GLA_HK ** -0.5)).astype(BF16)
    lo = _ROWS["alr"][0]
    alr = lax.dot_general(h, wt_ref[lo:lo + LANES, :], _NT, preferred_element_type=F32)
    kg_ref[...] = kg.astype(BF16)
    lane = lax.broadcasted_iota(jnp.int32, alr.shape, 1)
    alr = jnp.where(lane < GLA_GATE_RANK, alr, 0.0).astype(BF16)
    zg = jnp.dot(alr, wgg_ref[...], preferred_element_type=F32) + bgg_ref[...]
    vg = tok("vg")
    log_sig = jnp.minimum(zg, 0.0) - jnp.log1p(jnp.exp(-jnp.abs(zg)))
    la_ref[...] = log_sig * (1.0 / GLA_GATE_NORM)
    vg_ref[...] = vg.astype(BF16)


def _in_proj(x2, pos, freq, gin, wt, gq, wuqt, gkv, wuk, wuvt, wgg, bgg, tm):
    T = x2.shape[0]

    def row(w):
        return pl.BlockSpec((tm, w), lambda i: (i, 0))

    def colmajor(r):
        return pl.BlockSpec((r, tm), lambda i: (0, i))

    sds = jax.ShapeDtypeStruct
    out_shape = (sds((QPAD, T), BF16), sds((T, QPAD), BF16),
                 sds((MLA_WIDTH, T), BF16), sds((MLA_WIDTH, T), BF16),
                 sds((T, GLA_DK), BF16), sds((T, GLA_DK), BF16), sds((T, GLA_DV), BF16),
                 sds((T, GLA_DK), F32), sds((T, GLA_DV), BF16),
                 sds((T, D_MODEL), BF16), sds((T, D_MODEL), BF16))
    out_specs = (colmajor(QPAD), row(QPAD), colmajor(MLA_WIDTH), colmajor(MLA_WIDTH),
                 row(GLA_DK), row(GLA_DK), row(GLA_DV), row(GLA_DK), row(GLA_DV),
                 row(D_MODEL), row(D_MODEL))
    consts = (freq, gin, wt, gq, wuqt, gkv, wuk, wuvt, wgg, bgg)
    return pl.pallas_call(
        _in_proj_body,
        out_shape=out_shape,
        grid=(T // tm,),
        in_specs=[row(D_MODEL), colmajor(1)] + [_const_spec(c.shape) for c in consts],
        out_specs=out_specs,
        compiler_params=pltpu.CompilerParams(
            dimension_semantics=("arbitrary",), vmem_limit_bytes=VMEM_LIMIT),
        name="in_proj",
    )(x2, pos, *consts)


MLA_BUFS = 4


def _fold_rows(t, op):
    return op(t.reshape(t.shape[0] // SUBLANES, SUBLANES, t.shape[1]), axis=0)


def _mla_body(qt_ref, k_ref, vt_ref, o_ref, s_ref, p_ref, *, tq, nq):
    heads = 2
    row = lax.broadcasted_iota(jnp.int32, (tq, tq), 0)
    col = lax.broadcasted_iota(jnp.int32, (tq, tq), 1)
    keep = row <= col
    units = [(qi, hh) for qi in range(nq) for hh in range(heads)]
    stats = [dict() for _ in units]

    def tiles(qi):
        return [slice(t * tq, (t + 1) * tq) for t in range(qi + 1)]

    def scores(u):
        qi, hh = units[u]
        lanes = slice(hh * HEAD_PAD, (hh + 1) * HEAD_PAD)
        qh = qt_ref[lanes, qi * tq:(qi + 1) * tq]
        m8 = None
        for t, krows in enumerate(tiles(qi)):
            s = jnp.dot(k_ref[krows, lanes], qh, preferred_element_type=F32)
            if t == qi:
                s = jnp.where(keep, s, -jnp.inf)
            s_ref[u % MLA_BUFS, krows, :] = s
            tile_max = _fold_rows(s, jnp.max)
            m8 = tile_max if m8 is None else jnp.maximum(m8, tile_max)
            yield
        stats[u]["m"] = jnp.max(m8, axis=0, keepdims=True)

    def probs(u):
        qi, _ = units[u]
        m = stats[u]["m"]
        l8 = jnp.zeros((SUBLANES, tq), F32)
        for krows in tiles(qi):
            p = jnp.exp2(s_ref[u % MLA_BUFS, krows, :] - m)
            l8 = l8 + _fold_rows(p, jnp.sum)
            p_ref[u % MLA_BUFS, krows, :] = p.astype(BF16)
            yield
        stats[u]["l"] = jnp.sum(l8, axis=0, keepdims=True)

    def values(u):
        qi, hh = units[u]
        kv_len = (qi + 1) * tq
        vrows = slice(hh * MLA_VDIM, (hh + 1) * MLA_VDIM)
        o_t = jnp.dot(vt_ref[vrows, 0:kv_len], p_ref[u % MLA_BUFS, 0:kv_len, :],
                      preferred_element_type=F32)
        o_ref[vrows, qi * tq:(qi + 1) * tq] = (o_t / stats[u]["l"]).astype(BF16)
        yield

    stages = (scores, probs, values)
    for step in range(len(units) + len(stages) - 1):
        live = [stage(step - d) for d, stage in enumerate(stages)
                if 0 <= step - d < len(units)]
        while live:
            live = [g for g in live if next(g, StopIteration) is not StopIteration]


def _mla_attn(qt, k2, vt, B, S, tq):
    pairs = MLA_HEADS // 2
    return pl.pallas_call(
        functools.partial(_mla_body, tq=tq, nq=S // tq),
        out_shape=jax.ShapeDtypeStruct((MLA_WIDTH, B * S), BF16),
        grid=(B, pairs),
        in_specs=[pl.BlockSpec((2 * HEAD_PAD, S), lambda b, p: (p, b)),
                  pl.BlockSpec((S, 2 * HEAD_PAD), lambda b, p: (b, p)),
                  pl.BlockSpec((2 * MLA_VDIM, S), lambda b, p: (p, b))],
        out_specs=pl.BlockSpec((2 * MLA_VDIM, S), lambda b, p: (p, b)),
        scratch_shapes=[pltpu.VMEM((MLA_BUFS, S, tq), F32),
                        pltpu.VMEM((MLA_BUFS, S, tq), BF16)],
        compiler_params=pltpu.CompilerParams(
            dimension_semantics=("arbitrary", "arbitrary"), vmem_limit_bytes=VMEM_LIMIT),
        name="mla_attn",
    )(qt, k2, vt)


GLA_BLOCK = 256


def _gla_body(q_ref, k_ref, v_ref, la_ref, g_ref, o_ref, oi_ref, qin_ref, u_ref, dec_ref,
              *, seq):
    C, BLK = GLA_CHUNK, GLA_BLOCK
    per_blk = BLK // C
    r = lax.broadcasted_iota(jnp.int32, (BLK, BLK), 0)
    c = lax.broadcasted_iota(jnp.int32, (BLK, BLK), 1)
    causal = (c <= r) & (r // C == c // C)
    tril = jnp.where(causal, 1.0, 0.0).astype(BF16)
    gain = g_ref[...]

    chunk_of_row = lax.broadcasted_iota(jnp.int32, (BLK, GLA_HK), 0) // C
    n_blk = seq // BLK
    vals = [dict() for _ in range(n_blk)]
    carry = {"st": jnp.zeros((GLA_HV, GLA_HK), F32)}

    def decay(blk):
        g = la_ref[blk * BLK:(blk + 1) * BLK, :]
        g_hi = g.astype(BF16)
        g_lo = (g - g_hi.astype(F32)).astype(BF16)
        bb = jnp.dot(tril, jnp.concatenate([g_hi, g_lo], axis=1), preferred_element_type=F32)
        vals[blk]["b"] = bb[:, :GLA_HK] + bb[:, GLA_HK:]
        yield

    def scale(blk):
        rows = slice(blk * BLK, (blk + 1) * BLK)
        b = vals[blk].pop("b")
        lasts = [b[(j + 1) * C - 1:(j + 1) * C, :] for j in range(per_blk)]
        b_last = jnp.concatenate([jnp.broadcast_to(t, (C, GLA_HK)) for t in lasts], axis=0)
        q = q_ref[rows, :].astype(F32)
        k = k_ref[rows, :].astype(F32)
        q_in = (q * jnp.exp(b)).astype(BF16)
        k_in = (k * jnp.exp(-b)).astype(BF16)
        k_st = (k * jnp.exp(b_last - b)).astype(BF16)
        yield
        qin_ref[rows, :] = q_in
        for j in range(per_blk):
            dec_ref[blk * per_blk + j] = jnp.exp(lasts[j])
        zero = jnp.zeros_like(k_st)
        vals[blk]["k_bd"] = jnp.concatenate(
            [jnp.where(chunk_of_row == j, k_st, zero) for j in range(per_blk)], axis=1)
        vals[blk]["attn"] = lax.dot_general(q_in, k_in, _NT, preferred_element_type=F32)
        yield

    def mix(blk):
        rows = slice(blk * BLK, (blk + 1) * BLK)
        attn = jnp.where(causal, vals[blk].pop("attn"), 0.0).astype(BF16)
        v = v_ref[rows, :]
        oi_ref[rows, :] = jnp.dot(attn, v, preferred_element_type=F32)
        yield
        u_all = lax.dot_general(v, vals[blk].pop("k_bd"), _TN,
                                preferred_element_type=F32)
        for j in range(per_blk):
            u_ref[blk * per_blk + j] = u_all[:, j * GLA_HK:(j + 1) * GLA_HK]
        yield

    def scan(blk):
        for n in range(blk * per_blk, (blk + 1) * per_blk):
            crow = slice(n * C, (n + 1) * C)
            st = carry["st"]
            o = oi_ref[crow, :] + lax.dot_general(qin_ref[crow, :], st.astype(BF16), _NT,
                                                  preferred_element_type=F32)
            o_ref[crow, :] = _rms(o, gain).astype(BF16)
            carry["st"] = st * dec_ref[n] + u_ref[n]
            yield

    stages = (decay, scale, mix, scan)
    for step in range(n_blk + len(stages) - 1):
        live = [stage(step - d) for d, stage in enumerate(stages) if 0 <= step - d < n_blk]
        while live:
            live = [g for g in live if next(g, StopIteration) is not StopIteration]


def _gla(q3, k3, v3, la3, gain):
    B, S, _ = q3.shape
    n_chunks = S // GLA_CHUNK

    def spec(w):
        return pl.BlockSpec((None, S, w), lambda b, h: (b, 0, h))

    return pl.pallas_call(
        functools.partial(_gla_body, seq=S),
        out_shape=jax.ShapeDtypeStruct((B, S, GLA_DV), BF16),
        grid=(B, GLA_HEADS),
        in_specs=[spec(GLA_HK), spec(GLA_HK), spec(GLA_HV), spec(GLA_HK),
                  _const_spec(gain.shape)],
        out_specs=spec(GLA_HV),
        scratch_shapes=[pltpu.VMEM((S, GLA_HV), F32), pltpu.VMEM((S, GLA_HK), BF16),
                        pltpu.VMEM((n_chunks, GLA_HV, GLA_HK), F32),
                        pltpu.VMEM((n_chunks, 1, GLA_HK), F32)],
        compiler_params=pltpu.CompilerParams(
            dimension_semantics=("arbitrary", "arbitrary"), vmem_limit_bytes=VMEM_LIMIT),
        name="gla",
    )(q3, k3, v3, la3, gain)


def _out_body(x_ref, omt_ref, szmt_ref, og_ref, szg_ref, gm_ref, gg_ref,
              wpm_ref, wpg_ref, wo_ref, gf_ref, o_ref, *, final_norm):
    umt = (omt_ref[...].astype(F32) * szmt_ref[...].astype(F32)).astype(BF16)
    ug = (og_ref[...].astype(F32) * szg_ref[...].astype(F32)).astype(BF16)
    y_mla = lax.dot_general(umt, wpm_ref[...], _TN, preferred_element_type=F32)
    y_gla = jnp.dot(ug, wpg_ref[...], preferred_element_type=F32)
    merged = gm_ref[...].astype(F32) * y_mla + gg_ref[...].astype(F32) * y_gla
    r = x_ref[...] + jnp.dot(merged.astype(BF16), wo_ref[...], preferred_element_type=F32)
    o_ref[...] = _rms(r, gf_ref[...]) if final_norm else r


def _out_proj(x2, omt, szmt, og, szg, gm, gg, wpm, wpg, wo, gf, tm, final_norm):
    T = x2.shape[0]

    def row(w):
        return pl.BlockSpec((tm, w), lambda i: (i, 0))

    def colmajor(r):
        return pl.BlockSpec((r, tm), lambda i: (0, i))

    return pl.pallas_call(
        functools.partial(_out_body, final_norm=final_norm),
        out_shape=jax.ShapeDtypeStruct((T, D_MODEL), F32),
        grid=(T // tm,),
        in_specs=[row(D_MODEL), colmajor(MLA_WIDTH), colmajor(MLA_WIDTH), row(GLA_DV),
                  row(GLA_DV), row(D_MODEL), row(D_MODEL), _const_spec(wpm.shape),
                  _const_spec(wpg.shape), _const_spec(wo.shape), _const_spec(gf.shape)],
        out_specs=row(D_MODEL),
        compiler_params=pltpu.CompilerParams(
            dimension_semantics=("arbitrary",), vmem_limit_bytes=VMEM_LIMIT),
        name="out_proj",
    )(x2, omt, szmt, og, szg, gm, gg, wpm, wpg, wo, gf)


def _pad_heads(w, width):
    r = w.shape[0]
    w = w.reshape(r, MLA_HEADS, width)
    w = jnp.pad(w, ((0, 0), (0, 0), (0, HEAD_PAD - width)))
    return w.reshape(r, QPAD)


def kernel(x, positions, g_in, w_in, g_q, w_uq, g_kv, w_ukv, w_gla_gate, b_gla_gate,
           g_gla, w_proj_mla, w_proj_gla, w_out, g_final):
    B, S, D = x.shape
    T = B * S
    depth = w_in.shape[0]
    tm = 512

    freq = (ROPE_THETA ** (-jnp.arange(ROPE_HALF, dtype=F32) / ROPE_HALF)).reshape(ROPE_HALF, 1)
    pos = positions.astype(F32).reshape(1, T)

    x2 = x.reshape(T, D)
    for l in range(depth):
        wt = w_in[l].T.astype(BF16)
        wuqt = _pad_heads(w_uq[l] * (MLA_QK ** -0.5 * LOG2E), MLA_QK).T.astype(BF16)
        wukv = w_ukv[l].reshape(MLA_KV_RANK, MLA_HEADS, MLA_NOPE + MLA_VDIM)
        wuk = _pad_heads(wukv[:, :, :MLA_NOPE].reshape(MLA_KV_RANK, -1), MLA_NOPE).astype(BF16)
        wuvt = wukv[:, :, MLA_NOPE:].reshape(MLA_KV_RANK, MLA_WIDTH).T.astype(BF16)
        wgg = jnp.pad(w_gla_gate[l], ((0, LANES - GLA_GATE_RANK), (0, 0))).astype(BF16)

        (qt, k, vt, szmt, qg, kg, vg, la, szg, gm, gg) = _in_proj(
            x2, pos, freq, g_in[l].reshape(1, D), wt, g_q[l].reshape(1, -1), wuqt,
            g_kv[l].reshape(1, -1), wuk, wuvt, wgg, b_gla_gate[l].reshape(1, -1), tm)

        omt = _mla_attn(qt, k, vt, B, S, tq=256)
        o_gla = _gla(qg.reshape(B, S, GLA_DK), kg.reshape(B, S, GLA_DK),
                     vg.reshape(B, S, GLA_DV), la.reshape(B, S, GLA_DK),
                     g_gla[l].reshape(1, GLA_HV))

        x2 = _out_proj(x2, omt, szmt, o_gla.reshape(T, GLA_DV), szg,
                       gm, gg, w_proj_mla[l].astype(BF16), w_proj_gla[l].astype(BF16),
                       w_out[l].astype(BF16), g_final.reshape(1, D), tm,
                       final_norm=(l == depth - 1))
    return x2.reshape(B, S, D)
```

```python
import functools

import jax
import jax.numpy as jnp
from jax import lax
from jax.experimental import pallas as pl
from jax.experimental.pallas import tpu as pltpu

D_MODEL = 1024
EPS = 1e-6
MLA_HEADS = 8
MLA_NOPE = 64
MLA_ROPE = 32
MLA_VDIM = 64
MLA_Q_RANK = 384
MLA_KV_RANK = 256
MLA_QK = MLA_NOPE + MLA_ROPE
MLA_WIDTH = MLA_HEADS * MLA_VDIM
ROPE_THETA = 10000.0
GLA_HEADS = 4
GLA_DK = D_MODEL // 2
GLA_DV = D_MODEL
GLA_HK = GLA_DK // GLA_HEADS
GLA_HV = GLA_DV // GLA_HEADS
GLA_GATE_RANK = 16
GLA_GATE_NORM = 16.0
GLA_CHUNK = 64
SPLITS = (MLA_Q_RANK, MLA_KV_RANK, MLA_ROPE, MLA_WIDTH,
          GLA_DK, GLA_DK, GLA_DV, GLA_GATE_RANK, GLA_DV,
          D_MODEL, D_MODEL)

LANES = 128
SUBLANES = 8
HEAD_PAD = LANES
ROPE_HALF = MLA_ROPE // 2
ROPE_LO = MLA_NOPE
QPAD = MLA_HEADS * HEAD_PAD
LOW_RANK = MLA_Q_RANK + MLA_KV_RANK
LOG2E = 1.4426950408889634

_ROWS = {}
_o = 0
for _n, _w in zip(("cq", "ckv", "kr", "zmla", "qg", "kg", "vg", "alr", "zgla", "gmla", "ggla"),
                  SPLITS):
    _ROWS[_n] = (_o, _o + _w)
    _o += _w

VMEM_LIMIT = 52 * 1024 * 1024

BF16 = jnp.bfloat16
F32 = jnp.float32
_NT = (((1,), (1,)), ((), ()))
_TN = (((0,), (0,)), ((), ()))


def _const_spec(shape):
    nd = len(shape)
    return pl.BlockSpec(shape, lambda *_: (0,) * nd, pipeline_mode=pl.Buffered(1))


def _rms(v, g):
    return v * lax.rsqrt(jnp.mean(v * v, axis=-1, keepdims=True) + EPS) * g


def _in_proj_body(x_ref, pos_ref, freq_ref, gin_ref, wt_ref, wlr_ref, gq_ref, wuqt_ref,
                  gkv_ref, wuk_ref, wuvt_ref, wgg_ref, bgg_ref,
                  qt_ref, k_ref, vt_ref, szmt_ref, qg_ref, kg_ref, vg_ref, la_ref,
                  szg_ref, gm_ref, gg_ref):
    h = _rms(x_ref[...], gin_ref[...]).astype(BF16)
    tm = h.shape[0]

    def tok(name):
        lo, hi = _ROWS[name]
        return lax.dot_general(h, wt_ref[lo:hi, :], _NT, preferred_element_type=F32)

    def sigmoid_to(ref, z, gate=False):
        sg = jax.nn.sigmoid(z)
        ref[...] = (z * sg if gate else sg).astype(BF16)

    low = lax.dot_general(h, wlr_ref[...], _NT, preferred_element_type=F32)
    cq, ckv, alr = (low[:, :MLA_Q_RANK], low[:, MLA_Q_RANK:LOW_RANK],
                    low[:, LOW_RANK:].astype(BF16))
    qn = _rms(cq, gq_ref[...]).astype(BF16)
    lo = _ROWS["kr"][0]
    krz = lax.dot_general(wt_ref[lo:_ROWS["zmla"][1], :], h, _NT,
                          preferred_element_type=F32)
    kvn = _rms(ckv, gkv_ref[...]).astype(BF16)
    qt = lax.dot_general(wuqt_ref[...], qn, _NT, preferred_element_type=F32)

    ang = freq_ref[...] * pos_ref[...]
    cos, sin = jnp.cos(ang), jnp.sin(ang)

    def rope(x1, x2):
        return x1 * cos - x2 * sin, x2 * cos + x1 * sin

    zg = jnp.dot(alr, wgg_ref[...], preferred_element_type=F32) + bgg_ref[...]
    gm = tok("gmla")
    log_sig = jnp.minimum(zg, 0.0) - jnp.log(1.0 + jnp.exp(-jnp.abs(zg)))
    la_ref[...] = log_sig * (1.0 / GLA_GATE_NORM)
    for hd in range(MLA_HEADS):
        base = hd * HEAD_PAD
        lo = base + ROPE_LO
        r1, r2 = rope(qt[lo:lo + ROPE_HALF], qt[lo + ROPE_HALF:lo + MLA_ROPE])
        head = jnp.concatenate([qt[base:lo], r1, r2, qt[lo + MLA_ROPE:base + HEAD_PAD]], axis=0)
        qt_ref[base:base + HEAD_PAD, :] = head.astype(BF16)

    kn = jnp.dot(kvn, wuk_ref[...], preferred_element_type=F32)
    sigmoid_to(gm_ref, gm)
    gg = tok("ggla")
    r1, r2 = rope(krz[:ROPE_HALF], krz[ROPE_HALF:MLA_ROPE])
    kr = jnp.concatenate([jnp.zeros((ROPE_LO, tm), F32), r1, r2,
                          jnp.zeros((HEAD_PAD - ROPE_LO - MLA_ROPE, tm), F32)], axis=0).T
    for hd in range(MLA_HEADS):
        sl = slice(hd * HEAD_PAD, (hd + 1) * HEAD_PAD)
        k_ref[:, sl] = (kn[:, sl] + kr).astype(BF16)

    vt = lax.dot_general(wuvt_ref[...], kvn, _NT, preferred_element_type=F32)
    sigmoid_to(gg_ref, gg)
    zg_full = tok("zgla")
    vt_ref[...] = vt.astype(BF16)
    sigmoid_to(szmt_ref, krz[MLA_ROPE:], gate=True)
    qg = tok("qg")
    sigmoid_to(szg_ref, zg_full, gate=True)
    kg = tok("kg")
    qg_ref[...] = (qg * (GLA_HK ** -0.5)).astype(BF16)
    vg = tok("vg")
    kg_ref[...] = kg.astype(BF16)
    vg_ref[...] = vg.astype(BF16)


def _in_proj(x2, pos, freq, gin, wt, wlr, gq, wuqt, gkv, wuk, wuvt, wgg, bgg, tm):
    T = x2.shape[0]

    def row(w):
        return pl.BlockSpec((tm, w), lambda i: (i, 0))

    def colmajor(r):
        return pl.BlockSpec((r, tm), lambda i: (0, i))

    sds = jax.ShapeDtypeStruct
    out_shape = (sds((QPAD, T), BF16), sds((T, QPAD), BF16),
                 sds((MLA_WIDTH, T), BF16), sds((MLA_WIDTH, T), BF16),
                 sds((T, GLA_DK), BF16), sds((T, GLA_DK), BF16), sds((T, GLA_DV), BF16),
                 sds((T, GLA_DK), F32), sds((T, GLA_DV), BF16),
                 sds((T, D_MODEL), BF16), sds((T, D_MODEL), BF16))
    out_specs = (colmajor(QPAD), row(QPAD), colmajor(MLA_WIDTH), colmajor(MLA_WIDTH),
                 row(GLA_DK), row(GLA_DK), row(GLA_DV), row(GLA_DK), row(GLA_DV),
                 row(D_MODEL), row(D_MODEL))
    consts = (freq, gin, wt, wlr, gq, wuqt, gkv, wuk, wuvt, wgg, bgg)
    return pl.pallas_call(
        _in_proj_body,
        out_shape=out_shape,
        grid=(T // tm,),
        in_specs=[row(D_MODEL), colmajor(1)] + [_const_spec(c.shape) for c in consts],
        out_specs=out_specs,
        compiler_params=pltpu.CompilerParams(
            dimension_semantics=("arbitrary",), vmem_limit_bytes=VMEM_LIMIT),
        name="in_proj",
    )(x2, pos, *consts)


MLA_BUFS = 4
MLA_KEY_TILE = 256


def _fold_rows(t, op):
    return op(t.reshape(t.shape[0] // SUBLANES, SUBLANES, t.shape[1]), axis=0)


def _mla_body(qt_ref, k_ref, vt_ref, o_ref, s_ref, p_ref, *, tq, nq):
    heads = 2
    row = lax.broadcasted_iota(jnp.int32, (tq, tq), 0)
    col = lax.broadcasted_iota(jnp.int32, (tq, tq), 1)
    keep = row <= col
    units = [(qi, hh) for qi in reversed(range(nq)) for hh in range(heads)]
    stats = [dict() for _ in units]

    def tiles(qi):
        below = [slice(lo, min(lo + MLA_KEY_TILE, qi * tq))
                 for lo in range(0, qi * tq, MLA_KEY_TILE)]
        return below + [slice(qi * tq, (qi + 1) * tq)]

    def scores(u):
        qi, hh = units[u]
        lanes = slice(hh * HEAD_PAD, (hh + 1) * HEAD_PAD)
        qh = qt_ref[lanes, qi * tq:(qi + 1) * tq]
        m8 = None
        for krows in tiles(qi):
            s = jnp.dot(k_ref[krows, lanes], qh, preferred_element_type=F32)
            if krows.start == qi * tq:
                s = jnp.where(keep, s, -jnp.inf)
            s_ref[u % MLA_BUFS, krows, :] = s
            tile_max = _fold_rows(s, jnp.max)
            m8 = tile_max if m8 is None else jnp.maximum(m8, tile_max)
            yield
        stats[u]["m"] = jnp.max(m8, axis=0, keepdims=True)

    def probs(u):
        qi, _ = units[u]
        m = stats[u]["m"]
        l8 = jnp.zeros((SUBLANES, tq), F32)
        for krows in tiles(qi):
            p = jnp.exp2(s_ref[u % MLA_BUFS, krows, :] - m)
            l8 = l8 + _fold_rows(p, jnp.sum)
            p_ref[u % MLA_BUFS, krows, :] = p.astype(BF16)
            yield
        stats[u]["l"] = jnp.sum(l8, axis=0, keepdims=True)

    def values(u):
        qi, hh = units[u]
        kv_len = (qi + 1) * tq
        vrows = slice(hh * MLA_VDIM, (hh + 1) * MLA_VDIM)
        o_t = jnp.dot(vt_ref[vrows, 0:kv_len], p_ref[u % MLA_BUFS, 0:kv_len, :],
                      preferred_element_type=F32)
        o_ref[vrows, qi * tq:(qi + 1) * tq] = (o_t / stats[u]["l"]).astype(BF16)
        yield

    stages = (scores, probs, values)
    for step in range(len(units) + len(stages) - 1):
        live = [stage(step - d) for d, stage in enumerate(stages)
                if 0 <= step - d < len(units)]
        while live:
            live = [g for g in live if next(g, StopIteration) is not StopIteration]


def _mla_attn(qt, k2, vt, B, S, tq):
    pairs = MLA_HEADS // 2
    return pl.pallas_call(
        functools.partial(_mla_body, tq=tq, nq=S // tq),
        out_shape=jax.ShapeDtypeStruct((MLA_WIDTH, B * S), BF16),
        grid=(B, pairs),
        in_specs=[pl.BlockSpec((2 * HEAD_PAD, S), lambda b, p: (p, b)),
                  pl.BlockSpec((S, 2 * HEAD_PAD), lambda b, p: (b, p)),
                  pl.BlockSpec((2 * MLA_VDIM, S), lambda b, p: (p, b))],
        out_specs=pl.BlockSpec((2 * MLA_VDIM, S), lambda b, p: (p, b)),
        scratch_shapes=[pltpu.VMEM((MLA_BUFS, S, tq), F32),
                        pltpu.VMEM((MLA_BUFS, S, tq), BF16)],
        compiler_params=pltpu.CompilerParams(
            dimension_semantics=("arbitrary", "arbitrary"), vmem_limit_bytes=VMEM_LIMIT),
        name="mla_attn",
    )(qt, k2, vt)


GLA_BLOCK = 256
GLA_HEADS_PER_STEP = 2


def _gla_body(q_ref, k_ref, v_ref, la_ref, g_ref, o_ref, oi_ref, qin_ref, u_ref, dec_ref,
              *, seq, heads):
    C, BLK = GLA_CHUNK, GLA_BLOCK
    per_blk = BLK // C
    n_chunks = seq // C
    r = lax.broadcasted_iota(jnp.int32, (BLK, BLK), 0)
    c = lax.broadcasted_iota(jnp.int32, (BLK, BLK), 1)
    causal = (c <= r) & (r // C == c // C)
    tril = jnp.where(causal, 1.0, 0.0).astype(BF16)
    gain = g_ref[...]

    chunk_of_row = lax.broadcasted_iota(jnp.int32, (BLK, GLA_HK), 0) // C
    units = [(blk, hh) for blk in range(seq // BLK) for hh in range(heads)]
    vals = [dict() for _ in units]
    state = [jnp.zeros((GLA_HV, GLA_HK), F32) for _ in range(heads)]

    def kcols(hh):
        return slice(hh * GLA_HK, (hh + 1) * GLA_HK)

    def vcols(hh):
        return slice(hh * GLA_HV, (hh + 1) * GLA_HV)

    def decay(u):
        blk, hh = units[u]
        g = la_ref[blk * BLK:(blk + 1) * BLK, kcols(hh)]
        g_hi = g.astype(BF16)
        g_lo = (g - g_hi.astype(F32)).astype(BF16)
        bb = jnp.dot(tril, jnp.concatenate([g_hi, g_lo], axis=1), preferred_element_type=F32)
        vals[u]["b"] = bb[:, :GLA_HK] + bb[:, GLA_HK:]
        yield

    def scale(u):
        blk, hh = units[u]
        rows = slice(blk * BLK, (blk + 1) * BLK)
        b = vals[u].pop("b")
        lasts = [b[(j + 1) * C - 1:(j + 1) * C, :] for j in range(per_blk)]
        b_last = jnp.concatenate([jnp.broadcast_to(t, (C, GLA_HK)) for t in lasts], axis=0)
        q = q_ref[rows, kcols(hh)].astype(F32)
        k = k_ref[rows, kcols(hh)].astype(F32)
        q_in = (q * jnp.exp(b)).astype(BF16)
        k_in = (k * jnp.exp(-b)).astype(BF16)
        k_st = (k * jnp.exp(b_last - b)).astype(BF16)
        yield
        qin_ref[rows, kcols(hh)] = q_in
        for j in range(per_blk):
            dec_ref[hh * n_chunks + blk * per_blk + j] = jnp.exp(lasts[j])
        zero = jnp.zeros_like(k_st)
        vals[u]["k_bd"] = jnp.concatenate(
            [jnp.where(chunk_of_row == j, k_st, zero) for j in range(per_blk)], axis=1)
        vals[u]["attn"] = lax.dot_general(q_in, k_in, _NT, preferred_element_type=F32)
        yield

    def mix(u):
        blk, hh = units[u]
        rows = slice(blk * BLK, (blk + 1) * BLK)
        attn = jnp.where(causal, vals[u].pop("attn"), 0.0).astype(BF16)
        v = v_ref[rows, vcols(hh)]
        oi_ref[rows, vcols(hh)] = jnp.dot(attn, v, preferred_element_type=F32)
        yield
        u_all = lax.dot_general(v, vals[u].pop("k_bd"), _TN,
                                preferred_element_type=F32)
        for j in range(per_blk):
            u_ref[hh * n_chunks + blk * per_blk + j] = u_all[:, j * GLA_HK:(j + 1) * GLA_HK]
        yield

    def scan(u):
        blk, hh = units[u]
        for n in range(blk * per_blk, (blk + 1) * per_blk):
            crow = slice(n * C, (n + 1) * C)
            st = state[hh]
            o = oi_ref[crow, vcols(hh)] + lax.dot_general(
                qin_ref[crow, kcols(hh)], st.astype(BF16), _NT, preferred_element_type=F32)
            o_ref[crow, vcols(hh)] = _rms(o, gain).astype(BF16)
            state[hh] = st * dec_ref[hh * n_chunks + n] + u_ref[hh * n_chunks + n]
            yield

    stages = (decay, scale, mix, scan)
    for step in range(len(units) + len(stages) - 1):
        live = [stage(step - d) for d, stage in enumerate(stages)
                if 0 <= step - d < len(units)]
        while live:
            live = [g for g in live if next(g, StopIteration) is not StopIteration]


def _gla(q3, k3, v3, la3, gain):
    B, S, _ = q3.shape
    heads = GLA_HEADS_PER_STEP
    n_chunks = S // GLA_CHUNK

    def spec(w):
        return pl.BlockSpec((None, S, heads * w), lambda b, h: (b, 0, h))

    return pl.pallas_call(
        functools.partial(_gla_body, seq=S, heads=heads),
        out_shape=jax.ShapeDtypeStruct((B, S, GLA_DV), BF16),
        grid=(B, GLA_HEADS // heads),
        in_specs=[spec(GLA_HK), spec(GLA_HK), spec(GLA_HV), spec(GLA_HK),
                  _const_spec(gain.shape)],
        out_specs=spec(GLA_HV),
        scratch_shapes=[pltpu.VMEM((S, heads * GLA_HV), F32),
                        pltpu.VMEM((S, heads * GLA_HK), BF16),
                        pltpu.VMEM((heads * n_chunks, GLA_HV, GLA_HK), F32),
                        pltpu.VMEM((heads * n_chunks, 1, GLA_HK), F32)],
        compiler_params=pltpu.CompilerParams(
            dimension_semantics=("arbitrary", "arbitrary"), vmem_limit_bytes=VMEM_LIMIT),
        name="gla",
    )(q3, k3, v3, la3, gain)


def _out_body(x_ref, omt_ref, szmt_ref, og_ref, szg_ref, gm_ref, gg_ref,
              wpm_ref, wpg_ref, wo_ref, gf_ref, o_ref, *, final_norm):
    umt = (omt_ref[...].astype(F32) * szmt_ref[...].astype(F32)).astype(BF16)
    ug = (og_ref[...].astype(F32) * szg_ref[...].astype(F32)).astype(BF16)
    y_mla = lax.dot_general(umt, wpm_ref[...], _TN, preferred_element_type=F32)
    y_gla = jnp.dot(ug, wpg_ref[...], preferred_element_type=F32)
    merged = gm_ref[...].astype(F32) * y_mla + gg_ref[...].astype(F32) * y_gla
    r = x_ref[...] + jnp.dot(merged.astype(BF16), wo_ref[...], preferred_element_type=F32)
    o_ref[...] = _rms(r, gf_ref[...]) if final_norm else r


def _out_proj(x2, omt, szmt, og, szg, gm, gg, wpm, wpg, wo, gf, tm, final_norm):
    T = x2.shape[0]

    def row(w):
        return pl.BlockSpec((tm, w), lambda i: (i, 0))

    def colmajor(r):
        return pl.BlockSpec((r, tm), lambda i: (0, i))

    return pl.pallas_call(
        functools.partial(_out_body, final_norm=final_norm),
        out_shape=jax.ShapeDtypeStruct((T, D_MODEL), F32),
        grid=(T // tm,),
        in_specs=[row(D_MODEL), colmajor(MLA_WIDTH), colmajor(MLA_WIDTH), row(GLA_DV),
                  row(GLA_DV), row(D_MODEL), row(D_MODEL), _const_spec(wpm.shape),
                  _const_spec(wpg.shape), _const_spec(wo.shape), _const_spec(gf.shape)],
        out_specs=row(D_MODEL),
        compiler_params=pltpu.CompilerParams(
            dimension_semantics=("arbitrary",), vmem_limit_bytes=VMEM_LIMIT),
        name="out_proj",
    )(x2, omt, szmt, og, szg, gm, gg, wpm, wpg, wo, gf)


def _pad_heads(w, width):
    r = w.shape[0]
    w = w.reshape(r, MLA_HEADS, width)
    w = jnp.pad(w, ((0, 0), (0, 0), (0, HEAD_PAD - width)))
    return w.reshape(r, QPAD)


def kernel(x, positions, g_in, w_in, g_q, w_uq, g_kv, w_ukv, w_gla_gate, b_gla_gate,
           g_gla, w_proj_mla, w_proj_gla, w_out, g_final):
    B, S, D = x.shape
    T = B * S
    depth = w_in.shape[0]
    tm = 512

    freq = (ROPE_THETA ** (-jnp.arange(ROPE_HALF, dtype=F32) / ROPE_HALF)).reshape(ROPE_HALF, 1)
    pos = positions.astype(F32).reshape(1, T)

    x2 = x.reshape(T, D)
    for l in range(depth):
        wt = w_in[l].T.astype(BF16)
        lo = _ROWS["alr"][0]
        wlr = jnp.concatenate(
            [wt[:LOW_RANK], wt[lo:lo + GLA_GATE_RANK],
             jnp.zeros((LANES - GLA_GATE_RANK, D), BF16)], axis=0)
        wuqt = _pad_heads(w_uq[l] * (MLA_QK ** -0.5 * LOG2E), MLA_QK).T.astype(BF16)
        wukv = w_ukv[l].reshape(MLA_KV_RANK, MLA_HEADS, MLA_NOPE + MLA_VDIM)
        wuk = _pad_heads(wukv[:, :, :MLA_NOPE].reshape(MLA_KV_RANK, -1), MLA_NOPE).astype(BF16)
        wuvt = wukv[:, :, MLA_NOPE:].reshape(MLA_KV_RANK, MLA_WIDTH).T.astype(BF16)
        wgg = jnp.pad(w_gla_gate[l], ((0, LANES - GLA_GATE_RANK), (0, 0))).astype(BF16)

        (qt, k, vt, szmt, qg, kg, vg, la, szg, gm, gg) = _in_proj(
            x2, pos, freq, g_in[l].reshape(1, D), wt, wlr, g_q[l].reshape(1, -1), wuqt,
            g_kv[l].reshape(1, -1), wuk, wuvt, wgg, b_gla_gate[l].reshape(1, -1), tm)

        omt = _mla_attn(qt, k, vt, B, S, tq=256)
        o_gla = _gla(qg.reshape(B, S, GLA_DK), kg.reshape(B, S, GLA_DK),
                     vg.reshape(B, S, GLA_DV), la.reshape(B, S, GLA_DK),
                     g_gla[l].reshape(1, GLA_HV))

        x2 = _out_proj(x2, omt, szmt, o_gla.reshape(T, GLA_DV), szg,
                       gm, gg, w_proj_mla[l].astype(BF16), w_proj_gla[l].astype(BF16),
                       w_out[l].astype(BF16), g_final.reshape(1, D), tm,
                       final_norm=(l == depth - 1))
    return x2.reshape(B, S, D)
```

```python
import functools

import jax
import jax.numpy as jnp
from jax import lax
from jax.experimental import pallas as pl
from jax.experimental.pallas import tpu as pltpu

D_MODEL = 1024
EPS = 1e-6
MLA_HEADS = 8
MLA_NOPE = 64
MLA_ROPE = 32
MLA_VDIM = 64
MLA_Q_RANK = 384
MLA_KV_RANK = 256
MLA_QK = MLA_NOPE + MLA_ROPE
MLA_WIDTH = MLA_HEADS * MLA_VDIM
ROPE_THETA = 10000.0
GLA_HEADS = 4
GLA_DK = D_MODEL // 2
GLA_DV = D_MODEL
GLA_HK = GLA_DK // GLA_HEADS
GLA_HV = GLA_DV // GLA_HEADS
GLA_GATE_RANK = 16
GLA_GATE_NORM = 16.0
GLA_CHUNK = 64
SPLITS = (MLA_Q_RANK, MLA_KV_RANK, MLA_ROPE, MLA_WIDTH,
          GLA_DK, GLA_DK, GLA_DV, GLA_GATE_RANK, GLA_DV,
          D_MODEL, D_MODEL)

LANES = 128
SUBLANES = 8
HEAD_PAD = LANES
ROPE_HALF = MLA_ROPE // 2
ROPE_LO = MLA_NOPE
QPAD = MLA_HEADS * HEAD_PAD
LOW_RANK = MLA_Q_RANK + MLA_KV_RANK
LOG2E = 1.4426950408889634

_ROWS = {}
_o = 0
for _n, _w in zip(("cq", "ckv", "kr", "zmla", "qg", "kg", "vg", "alr", "zgla", "gmla", "ggla"),
                  SPLITS):
    _ROWS[_n] = (_o, _o + _w)
    _o += _w

VMEM_LIMIT = 52 * 1024 * 1024

BF16 = jnp.bfloat16
F32 = jnp.float32
_NT = (((1,), (1,)), ((), ()))
_TN = (((0,), (0,)), ((), ()))


def _const_spec(shape):
    nd = len(shape)
    return pl.BlockSpec(shape, lambda *_: (0,) * nd, pipeline_mode=pl.Buffered(1))


def _rms(v, g):
    return v * lax.rsqrt(jnp.mean(v * v, axis=-1, keepdims=True) + EPS) * g


def _in_proj_body(x_ref, pos_ref, freq_ref, gin_ref, wt_ref, wlr_ref, gq_ref, wuqt_ref,
                  gkv_ref, wuk_ref, wuvt_ref, wgg_ref, bgg_ref,
                  qt_ref, k_ref, vt_ref, szmt_ref, qg_ref, kg_ref, vg_ref, la_ref,
                  szg_ref, gm_ref, gg_ref):
    h = _rms(x_ref[...], gin_ref[...]).astype(BF16)
    tm = h.shape[0]

    def tok(name):
        lo, hi = _ROWS[name]
        return lax.dot_general(h, wt_ref[lo:hi, :], _NT, preferred_element_type=F32)

    def sigmoid_to(ref, z, gate=False):
        sg = jax.nn.sigmoid(z)
        ref[...] = (z * sg if gate else sg).astype(BF16)

    low = lax.dot_general(h, wlr_ref[...], _NT, preferred_element_type=F32)
    cq, ckv, alr = (low[:, :MLA_Q_RANK], low[:, MLA_Q_RANK:LOW_RANK],
                    low[:, LOW_RANK:].astype(BF16))
    qn = _rms(cq, gq_ref[...]).astype(BF16)
    lo = _ROWS["kr"][0]
    krz = lax.dot_general(wt_ref[lo:_ROWS["zmla"][1], :], h, _NT,
                          preferred_element_type=F32)
    kvn = _rms(ckv, gkv_ref[...]).astype(BF16)
    qt = lax.dot_general(wuqt_ref[...], qn, _NT, preferred_element_type=F32)

    ang = freq_ref[...] * pos_ref[...]
    cos, sin = jnp.cos(ang), jnp.sin(ang)

    def rope(x1, x2):
        return x1 * cos - x2 * sin, x2 * cos + x1 * sin

    zg = jnp.dot(alr, wgg_ref[...], preferred_element_type=F32) + bgg_ref[...]
    gm = tok("gmla")
    log_sig = jnp.minimum(zg, 0.0) - jnp.log(1.0 + jnp.exp(-jnp.abs(zg)))
    la_ref[...] = log_sig * (1.0 / GLA_GATE_NORM)
    for hd in range(MLA_HEADS):
        base = hd * HEAD_PAD
        lo = base + ROPE_LO
        r1, r2 = rope(qt[lo:lo + ROPE_HALF], qt[lo + ROPE_HALF:lo + MLA_ROPE])
        head = jnp.concatenate([qt[base:lo], r1, r2, qt[lo + MLA_ROPE:base + HEAD_PAD]], axis=0)
        qt_ref[base:base + HEAD_PAD, :] = head.astype(BF16)

    kn = jnp.dot(kvn, wuk_ref[...], preferred_element_type=F32)
    sigmoid_to(gm_ref, gm)
    gg = tok("ggla")
    r1, r2 = rope(krz[:ROPE_HALF], krz[ROPE_HALF:MLA_ROPE])
    kr = jnp.concatenate([jnp.zeros((ROPE_LO, tm), F32), r1, r2,
                          jnp.zeros((HEAD_PAD - ROPE_LO - MLA_ROPE, tm), F32)], axis=0).T
    for hd in range(MLA_HEADS):
        sl = slice(hd * HEAD_PAD, (hd + 1) * HEAD_PAD)
        k_ref[:, sl] = (kn[:, sl] + kr).astype(BF16)

    vt = lax.dot_general(wuvt_ref[...], kvn, _NT, preferred_element_type=F32)
    sigmoid_to(gg_ref, gg)
    zg_full = tok("zgla")
    vt_ref[...] = vt.astype(BF16)
    sigmoid_to(szmt_ref, krz[MLA_ROPE:], gate=True)
    qg = tok("qg")
    sigmoid_to(szg_ref, zg_full, gate=True)
    kg = tok("kg")
    qg_ref[...] = (qg * (GLA_HK ** -0.5)).astype(BF16)
    vg = tok("vg")
    kg_ref[...] = kg.astype(BF16)
    vg_ref[...] = vg.astype(BF16)


def _in_proj(x2, pos, freq, gin, wt, wlr, gq, wuqt, gkv, wuk, wuvt, wgg, bgg, tm):
    T = x2.shape[0]

    def row(w):
        return pl.BlockSpec((tm, w), lambda i: (i, 0))

    def colmajor(r):
        return pl.BlockSpec((r, tm), lambda i: (0, i))

    sds = jax.ShapeDtypeStruct
    out_shape = (sds((QPAD, T), BF16), sds((T, QPAD), BF16),
                 sds((MLA_WIDTH, T), BF16), sds((MLA_WIDTH, T), BF16),
                 sds((T, GLA_DK), BF16), sds((T, GLA_DK), BF16), sds((T, GLA_DV), BF16),
                 sds((T, GLA_DK), F32), sds((T, GLA_DV), BF16),
                 sds((T, D_MODEL), BF16), sds((T, D_MODEL), BF16))
    out_specs = (colmajor(QPAD), row(QPAD), colmajor(MLA_WIDTH), colmajor(MLA_WIDTH),
                 row(GLA_DK), row(GLA_DK), row(GLA_DV), row(GLA_DK), row(GLA_DV),
                 row(D_MODEL), row(D_MODEL))
    consts = (freq, gin, wt, wlr, gq, wuqt, gkv, wuk, wuvt, wgg, bgg)
    return pl.pallas_call(
        _in_proj_body,
        out_shape=out_shape,
        grid=(T // tm,),
        in_specs=[row(D_MODEL), colmajor(1)] + [_const_spec(c.shape) for c in consts],
        out_specs=out_specs,
        compiler_params=pltpu.CompilerParams(
            dimension_semantics=("arbitrary",), vmem_limit_bytes=VMEM_LIMIT),
        name="in_proj",
    )(x2, pos, *consts)


MLA_BUFS = 4
MLA_KEY_TILE = 256
MLA_HEADS_PER_STEP = 4


def _fold_rows(t, op):
    return op(t.reshape(t.shape[0] // SUBLANES, SUBLANES, t.shape[1]), axis=0)


def _mla_body(qt_ref, k_ref, vt_ref, o_ref, s_ref, p_ref, *, tq, nq, heads):
    row = lax.broadcasted_iota(jnp.int32, (tq, tq), 0)
    col = lax.broadcasted_iota(jnp.int32, (tq, tq), 1)
    keep = row <= col
    units = [(qi, hh) for qi in reversed(range(nq)) for hh in range(heads)]
    stats = [dict() for _ in units]

    def tiles(qi):
        below = [slice(lo, min(lo + MLA_KEY_TILE, qi * tq))
                 for lo in range(0, qi * tq, MLA_KEY_TILE)]
        return below + [slice(qi * tq, (qi + 1) * tq)]

    def scores(u):
        qi, hh = units[u]
        lanes = slice(hh * HEAD_PAD, (hh + 1) * HEAD_PAD)
        qh = qt_ref[lanes, qi * tq:(qi + 1) * tq]
        m8 = None
        for krows in tiles(qi):
            s = jnp.dot(k_ref[krows, lanes], qh, preferred_element_type=F32)
            if krows.start == qi * tq:
                s = jnp.where(keep, s, -jnp.inf)
            s_ref[u % MLA_BUFS, krows, :] = s
            tile_max = _fold_rows(s, jnp.max)
            m8 = tile_max if m8 is None else jnp.maximum(m8, tile_max)
            yield
        stats[u]["m"] = jnp.max(m8, axis=0, keepdims=True)

    def probs(u):
        qi, _ = units[u]
        m = stats[u]["m"]
        l8 = jnp.zeros((SUBLANES, tq), F32)
        for krows in tiles(qi):
            p = jnp.exp2(s_ref[u % MLA_BUFS, krows, :] - m)
            l8 = l8 + _fold_rows(p, jnp.sum)
            p_ref[u % MLA_BUFS, krows, :] = p.astype(BF16)
            yield
        stats[u]["l"] = jnp.sum(l8, axis=0, keepdims=True)

    def values(u):
        qi, hh = units[u]
        kv_len = (qi + 1) * tq
        vrows = slice(hh * MLA_VDIM, (hh + 1) * MLA_VDIM)
        o_t = jnp.dot(vt_ref[vrows, 0:kv_len], p_ref[u % MLA_BUFS, 0:kv_len, :],
                      preferred_element_type=F32)
        o_ref[vrows, qi * tq:(qi + 1) * tq] = (o_t / stats[u]["l"]).astype(BF16)
        yield

    stages = (scores, probs, values)
    for step in range(len(units) + len(stages) - 1):
        live = [stage(step - d) for d, stage in enumerate(stages)
                if 0 <= step - d < len(units)]
        while live:
            live = [g for g in live if next(g, StopIteration) is not StopIteration]


def _mla_attn(qt, k2, vt, B, S, tq):
    heads = MLA_HEADS_PER_STEP
    return pl.pallas_call(
        functools.partial(_mla_body, tq=tq, nq=S // tq, heads=heads),
        out_shape=jax.ShapeDtypeStruct((MLA_WIDTH, B * S), BF16),
        grid=(B, MLA_HEADS // heads),
        in_specs=[pl.BlockSpec((heads * HEAD_PAD, S), lambda b, p: (p, b)),
                  pl.BlockSpec((S, heads * HEAD_PAD), lambda b, p: (b, p)),
                  pl.BlockSpec((heads * MLA_VDIM, S), lambda b, p: (p, b))],
        out_specs=pl.BlockSpec((heads * MLA_VDIM, S), lambda b, p: (p, b)),
        scratch_shapes=[pltpu.VMEM((MLA_BUFS, S, tq), F32),
                        pltpu.VMEM((MLA_BUFS, S, tq), BF16)],
        compiler_params=pltpu.CompilerParams(
            dimension_semantics=("arbitrary", "arbitrary"), vmem_limit_bytes=VMEM_LIMIT),
        name="mla_attn",
    )(qt, k2, vt)


GLA_BLOCK = 256
GLA_HEADS_PER_STEP = 2


def _gla_body(q_ref, k_ref, v_ref, la_ref, g_ref, o_ref, oi_ref, qin_ref, u_ref, dec_ref,
              *, seq, heads):
    C, BLK = GLA_CHUNK, GLA_BLOCK
    per_blk = BLK // C
    n_chunks = seq // C
    r = lax.broadcasted_iota(jnp.int32, (BLK, BLK), 0)
    c = lax.broadcasted_iota(jnp.int32, (BLK, BLK), 1)
    causal = (c <= r) & (r // C == c // C)
    tril = jnp.where(causal, 1.0, 0.0).astype(BF16)
    gain = g_ref[...]

    chunk_of_row = lax.broadcasted_iota(jnp.int32, (BLK, GLA_HK), 0) // C
    units = [(blk, hh) for blk in range(seq // BLK) for hh in range(heads)]
    vals = [dict() for _ in units]
    state = [jnp.zeros((GLA_HV, GLA_HK), F32) for _ in range(heads)]

    def kcols(hh):
        return slice(hh * GLA_HK, (hh + 1) * GLA_HK)

    def vcols(hh):
        return slice(hh * GLA_HV, (hh + 1) * GLA_HV)

    def decay(u):
        blk, hh = units[u]
        g = la_ref[blk * BLK:(blk + 1) * BLK, kcols(hh)]
        g_hi = g.astype(BF16)
        g_lo = (g - g_hi.astype(F32)).astype(BF16)
        bb = jnp.dot(tril, jnp.concatenate([g_hi, g_lo], axis=1), preferred_element_type=F32)
        vals[u]["b"] = bb[:, :GLA_HK] + bb[:, GLA_HK:]
        yield

    def scale(u):
        blk, hh = units[u]
        rows = slice(blk * BLK, (blk + 1) * BLK)
        b = vals[u].pop("b")
        lasts = [b[(j + 1) * C - 1:(j + 1) * C, :] for j in range(per_blk)]
        b_last = jnp.concatenate([jnp.broadcast_to(t, (C, GLA_HK)) for t in lasts], axis=0)
        q = q_ref[rows, kcols(hh)].astype(F32)
        k = k_ref[rows, kcols(hh)].astype(F32)
        q_in = (q * jnp.exp(b)).astype(BF16)
        k_in = (k * jnp.exp(-b)).astype(BF16)
        k_st = (k * jnp.exp(b_last - b)).astype(BF16)
        yield
        qin_ref[rows, kcols(hh)] = q_in
        for j in range(per_blk):
            dec_ref[hh * n_chunks + blk * per_blk + j] = jnp.exp(lasts[j])
        zero = jnp.zeros_like(k_st)
        vals[u]["k_bd"] = jnp.concatenate(
            [jnp.where(chunk_of_row == j, k_st, zero) for j in range(per_blk)], axis=1)
        vals[u]["attn"] = lax.dot_general(q_in, k_in, _NT, preferred_element_type=F32)
        yield

    def mix(u):
        blk, hh = units[u]
        rows = slice(blk * BLK, (blk + 1) * BLK)
        attn = jnp.where(causal, vals[u].pop("attn"), 0.0).astype(BF16)
        v = v_ref[rows, vcols(hh)]
        oi_ref[rows, vcols(hh)] = jnp.dot(attn, v, preferred_element_type=F32)
        yield
        u_all = lax.dot_general(v, vals[u].pop("k_bd"), _TN,
                                preferred_element_type=F32)
        for j in range(per_blk):
            u_ref[hh * n_chunks + blk * per_blk + j] = u_all[:, j * GLA_HK:(j + 1) * GLA_HK]
        yield

    def scan(u):
        blk, hh = units[u]
        for n in range(blk * per_blk, (blk + 1) * per_blk):
            crow = slice(n * C, (n + 1) * C)
            st = state[hh]
            o = oi_ref[crow, vcols(hh)] + lax.dot_general(
                qin_ref[crow, kcols(hh)], st.astype(BF16), _NT, preferred_element_type=F32)
            o_ref[crow, vcols(hh)] = _rms(o, gain).astype(BF16)
            state[hh] = st * dec_ref[hh * n_chunks + n] + u_ref[hh * n_chunks + n]
            yield

    stages = (decay, scale, mix, scan)
    for step in range(len(units) + len(stages) - 1):
        live = [stage(step - d) for d, stage in enumerate(stages)
                if 0 <= step - d < len(units)]
        while live:
            live = [g for g in live if next(g, StopIteration) is not StopIteration]


def _gla(q3, k3, v3, la3, gain):
    B, S, _ = q3.shape
    heads = GLA_HEADS_PER_STEP
    n_chunks = S // GLA_CHUNK

    def spec(w):
        return pl.BlockSpec((None, S, heads * w), lambda b, h: (b, 0, h))

    return pl.pallas_call(
        functools.partial(_gla_body, seq=S, heads=heads),
        out_shape=jax.ShapeDtypeStruct((B, S, GLA_DV), BF16),
        grid=(B, GLA_HEADS // heads),
        in_specs=[spec(GLA_HK), spec(GLA_HK), spec(GLA_HV), spec(GLA_HK),
                  _const_spec(gain.shape)],
        out_specs=spec(GLA_HV),
        scratch_shapes=[pltpu.VMEM((S, heads * GLA_HV), F32),
                        pltpu.VMEM((S, heads * GLA_HK), BF16),
                        pltpu.VMEM((heads * n_chunks, GLA_HV, GLA_HK), F32),
                        pltpu.VMEM((heads * n_chunks, 1, GLA_HK), F32)],
        compiler_params=pltpu.CompilerParams(
            dimension_semantics=("arbitrary", "arbitrary"), vmem_limit_bytes=VMEM_LIMIT),
        name="gla",
    )(q3, k3, v3, la3, gain)


def _out_body(x_ref, omt_ref, szmt_ref, og_ref, szg_ref, gm_ref, gg_ref,
              wpm_ref, wpg_ref, wo_ref, gf_ref, o_ref, *, final_norm):
    umt = (omt_ref[...].astype(F32) * szmt_ref[...].astype(F32)).astype(BF16)
    ug = (og_ref[...].astype(F32) * szg_ref[...].astype(F32)).astype(BF16)
    y_mla = lax.dot_general(umt, wpm_ref[...], _TN, preferred_element_type=F32)
    y_gla = jnp.dot(ug, wpg_ref[...], preferred_element_type=F32)
    merged = gm_ref[...].astype(F32) * y_mla + gg_ref[...].astype(F32) * y_gla
    r = x_ref[...] + jnp.dot(merged.astype(BF16), wo_ref[...], preferred_element_type=F32)
    o_ref[...] = _rms(r, gf_ref[...]) if final_norm else r


def _out_proj(x2, omt, szmt, og, szg, gm, gg, wpm, wpg, wo, gf, tm, final_norm):
    T = x2.shape[0]

    def row(w):
        return pl.BlockSpec((tm, w), lambda i: (i, 0))

    def colmajor(r):
        return pl.BlockSpec((r, tm), lambda i: (0, i))

    return pl.pallas_call(
        functools.partial(_out_body, final_norm=final_norm),
        out_shape=jax.ShapeDtypeStruct((T, D_MODEL), F32),
        grid=(T // tm,),
        in_specs=[row(D_MODEL), colmajor(MLA_WIDTH), colmajor(MLA_WIDTH), row(GLA_DV),
                  row(GLA_DV), row(D_MODEL), row(D_MODEL), _const_spec(wpm.shape),
                  _const_spec(wpg.shape), _const_spec(wo.shape), _const_spec(gf.shape)],
        out_specs=row(D_MODEL),
        compiler_params=pltpu.CompilerParams(
            dimension_semantics=("arbitrary",), vmem_limit_bytes=VMEM_LIMIT),
        name="out_proj",
    )(x2, omt, szmt, og, szg, gm, gg, wpm, wpg, wo, gf)


def _pad_heads(w, width):
    r = w.shape[0]
    w = w.reshape(r, MLA_HEADS, width)
    w = jnp.pad(w, ((0, 0), (0, 0), (0, HEAD_PAD - width)))
    return w.reshape(r, QPAD)


def kernel(x, positions, g_in, w_in, g_q, w_uq, g_kv, w_ukv, w_gla_gate, b_gla_gate,
           g_gla, w_proj_mla, w_proj_gla, w_out, g_final):
    B, S, D = x.shape
    T = B * S
    depth = w_in.shape[0]
    tm = 512

    freq = (ROPE_THETA ** (-jnp.arange(ROPE_HALF, dtype=F32) / ROPE_HALF)).reshape(ROPE_HALF, 1)
    pos = positions.astype(F32).reshape(1, T)

    x2 = x.reshape(T, D)
    for l in range(depth):
        wt = w_in[l].T.astype(BF16)
        lo = _ROWS["alr"][0]
        wlr = jnp.concatenate(
            [wt[:LOW_RANK], wt[lo:lo + GLA_GATE_RANK],
             jnp.zeros((LANES - GLA_GATE_RANK, D), BF16)], axis=0)
        wuqt = _pad_heads(w_uq[l] * (MLA_QK ** -0.5 * LOG2E), MLA_QK).T.astype(BF16)
        wukv = w_ukv[l].reshape(MLA_KV_RANK, MLA_HEADS, MLA_NOPE + MLA_VDIM)
        wuk = _pad_heads(wukv[:, :, :MLA_NOPE].reshape(MLA_KV_RANK, -1), MLA_NOPE).astype(BF16)
        wuvt = wukv[:, :, MLA_NOPE:].reshape(MLA_KV_RANK, MLA_WIDTH).T.astype(BF16)
        wgg = jnp.pad(w_gla_gate[l], ((0, LANES - GLA_GATE_RANK), (0, 0))).astype(BF16)

        (qt, k, vt, szmt, qg, kg, vg, la, szg, gm, gg) = _in_proj(
            x2, pos, freq, g_in[l].reshape(1, D), wt, wlr, g_q[l].reshape(1, -1), wuqt,
            g_kv[l].reshape(1, -1), wuk, wuvt, wgg, b_gla_gate[l].reshape(1, -1), tm)

        omt = _mla_attn(qt, k, vt, B, S, tq=256)
        o_gla = _gla(qg.reshape(B, S, GLA_DK), kg.reshape(B, S, GLA_DK),
                     vg.reshape(B, S, GLA_DV), la.reshape(B, S, GLA_DK),
                     g_gla[l].reshape(1, GLA_HV))

        x2 = _out_proj(x2, omt, szmt, o_gla.reshape(T, GLA_DV), szg,
                       gm, gg, w_proj_mla[l].astype(BF16), w_proj_gla[l].astype(BF16),
                       w_out[l].astype(BF16), g_final.reshape(1, D), tm,
                       final_norm=(l == depth - 1))
    return x2.reshape(B, S, D)
```

```python
import functools

import jax
import jax.numpy as jnp
from jax import lax
from jax.experimental import pallas as pl
from jax.experimental.pallas import tpu as pltpu

D_MODEL = 1024
EPS = 1e-6
MLA_HEADS = 8
MLA_NOPE = 64
MLA_ROPE = 32
MLA_VDIM = 64
MLA_Q_RANK = 384
MLA_KV_RANK = 256
MLA_QK = MLA_NOPE + MLA_ROPE
MLA_WIDTH = MLA_HEADS * MLA_VDIM
ROPE_THETA = 10000.0
GLA_HEADS = 4
GLA_DK = D_MODEL // 2
GLA_DV = D_MODEL
GLA_HK = GLA_DK // GLA_HEADS
GLA_HV = GLA_DV // GLA_HEADS
GLA_GATE_RANK = 16
GLA_GATE_NORM = 16.0
GLA_CHUNK = 64
SPLITS = (MLA_Q_RANK, MLA_KV_RANK, MLA_ROPE, MLA_WIDTH,
          GLA_DK, GLA_DK, GLA_DV, GLA_GATE_RANK, GLA_DV,
          D_MODEL, D_MODEL)

LANES = 128
SUBLANES = 8
HEAD_PAD = LANES
ROPE_HALF = MLA_ROPE // 2
ROPE_LO = MLA_NOPE
QPAD = MLA_HEADS * HEAD_PAD
LOW_RANK = MLA_Q_RANK + MLA_KV_RANK
LOG2E = 1.4426950408889634

_ROWS = {}
_o = 0
for _n, _w in zip(("cq", "ckv", "kr", "zmla", "qg", "kg", "vg", "alr", "zgla", "gmla", "ggla"),
                  SPLITS):
    _ROWS[_n] = (_o, _o + _w)
    _o += _w

VMEM_LIMIT = 52 * 1024 * 1024

BF16 = jnp.bfloat16
F32 = jnp.float32
_NT = (((1,), (1,)), ((), ()))
_TN = (((0,), (0,)), ((), ()))


def _const_spec(shape):
    nd = len(shape)
    return pl.BlockSpec(shape, lambda *_: (0,) * nd, pipeline_mode=pl.Buffered(1))


def _rms(v, g):
    return v * lax.rsqrt(jnp.mean(v * v, axis=-1, keepdims=True) + EPS) * g


def _in_proj_body(x_ref, pos_ref, freq_ref, gin_ref, wt_ref, wlr_ref, gq_ref, wuqt_ref,
                  gkv_ref, wuk_ref, wuvt_ref, wgg_ref, bgg_ref,
                  qt_ref, k_ref, vt_ref, szmt_ref, qg_ref, kg_ref, vg_ref, la_ref,
                  szg_ref, gm_ref, gg_ref):
    h = _rms(x_ref[...], gin_ref[...]).astype(BF16)
    tm = h.shape[0]

    def tok(name):
        lo, hi = _ROWS[name]
        return lax.dot_general(h, wt_ref[lo:hi, :], _NT, preferred_element_type=F32)

    def sigmoid_to(ref, z, gate=False):
        sg = 0.5 * jnp.tanh(0.5 * z) + 0.5
        ref[...] = (z * sg if gate else sg).astype(BF16)

    low = lax.dot_general(h, wlr_ref[...], _NT, preferred_element_type=F32)
    cq, ckv, alr = (low[:, :MLA_Q_RANK], low[:, MLA_Q_RANK:LOW_RANK],
                    low[:, LOW_RANK:].astype(BF16))
    qn = _rms(cq, gq_ref[...]).astype(BF16)
    lo = _ROWS["kr"][0]
    krz = lax.dot_general(wt_ref[lo:_ROWS["zmla"][1], :], h, _NT,
                          preferred_element_type=F32)
    kvn = _rms(ckv, gkv_ref[...]).astype(BF16)
    qt = lax.dot_general(wuqt_ref[...], qn, _NT, preferred_element_type=F32)

    ang = freq_ref[...] * pos_ref[...]
    cos, sin = jnp.cos(ang), jnp.sin(ang)

    def rope(x1, x2):
        return x1 * cos - x2 * sin, x2 * cos + x1 * sin

    zg = jnp.dot(alr, wgg_ref[...], preferred_element_type=F32) + bgg_ref[...]
    gm = tok("gmla")
    log_sig = jnp.minimum(zg, 0.0) - jnp.log(1.0 + jnp.exp(-jnp.abs(zg)))
    la_ref[...] = log_sig * (1.0 / GLA_GATE_NORM)
    for hd in range(MLA_HEADS):
        base = hd * HEAD_PAD
        lo = base + ROPE_LO
        r1, r2 = rope(qt[lo:lo + ROPE_HALF], qt[lo + ROPE_HALF:lo + MLA_ROPE])
        head = jnp.concatenate([qt[base:lo], r1, r2, qt[lo + MLA_ROPE:base + HEAD_PAD]], axis=0)
        qt_ref[base:base + HEAD_PAD, :] = head.astype(BF16)

    kn = jnp.dot(kvn, wuk_ref[...], preferred_element_type=F32)
    sigmoid_to(gm_ref, gm)
    gg = tok("ggla")
    r1, r2 = rope(krz[:ROPE_HALF], krz[ROPE_HALF:MLA_ROPE])
    kr = jnp.concatenate([jnp.zeros((ROPE_LO, tm), F32), r1, r2,
                          jnp.zeros((HEAD_PAD - ROPE_LO - MLA_ROPE, tm), F32)], axis=0).T
    for hd in range(MLA_HEADS):
        sl = slice(hd * HEAD_PAD, (hd + 1) * HEAD_PAD)
        k_ref[:, sl] = (kn[:, sl] + kr).astype(BF16)

    vt = lax.dot_general(wuvt_ref[...], kvn, _NT, preferred_element_type=F32)
    sigmoid_to(gg_ref, gg)
    zg_full = tok("zgla")
    vt_ref[...] = vt.astype(BF16)
    sigmoid_to(szmt_ref, krz[MLA_ROPE:], gate=True)
    qg = tok("qg")
    sigmoid_to(szg_ref, zg_full, gate=True)
    kg = tok("kg")
    qg_ref[...] = qg * (GLA_HK ** -0.5)
    vg = tok("vg")
    kg_ref[...] = kg
    vg_ref[...] = vg.astype(BF16)


def _in_proj(x2, pos, freq, gin, wt, wlr, gq, wuqt, gkv, wuk, wuvt, wgg, bgg, tm):
    T = x2.shape[0]

    def row(w):
        return pl.BlockSpec((tm, w), lambda i: (i, 0))

    def colmajor(r):
        return pl.BlockSpec((r, tm), lambda i: (0, i))

    sds = jax.ShapeDtypeStruct
    out_shape = (sds((QPAD, T), BF16), sds((T, QPAD), BF16),
                 sds((MLA_WIDTH, T), BF16), sds((MLA_WIDTH, T), BF16),
                 sds((T, GLA_DK), F32), sds((T, GLA_DK), F32), sds((T, GLA_DV), BF16),
                 sds((T, GLA_DK), F32), sds((T, GLA_DV), BF16),
                 sds((T, D_MODEL), BF16), sds((T, D_MODEL), BF16))
    out_specs = (colmajor(QPAD), row(QPAD), colmajor(MLA_WIDTH), colmajor(MLA_WIDTH),
                 row(GLA_DK), row(GLA_DK), row(GLA_DV), row(GLA_DK), row(GLA_DV),
                 row(D_MODEL), row(D_MODEL))
    consts = (freq, gin, wt, wlr, gq, wuqt, gkv, wuk, wuvt, wgg, bgg)
    return pl.pallas_call(
        _in_proj_body,
        out_shape=out_shape,
        grid=(T // tm,),
        in_specs=[row(D_MODEL), colmajor(1)] + [_const_spec(c.shape) for c in consts],
        out_specs=out_specs,
        compiler_params=pltpu.CompilerParams(
            dimension_semantics=("arbitrary",), vmem_limit_bytes=VMEM_LIMIT),
        name="in_proj",
    )(x2, pos, *consts)


MLA_BUFS = 4
MLA_KEY_TILE = 256
MLA_HEADS_PER_STEP = 4


def _fold_rows(t, op):
    return op(t.reshape(t.shape[0] // SUBLANES, SUBLANES, t.shape[1]), axis=0)


def _mla_body(qt_ref, k_ref, vt_ref, o_ref, s_ref, p_ref, *, tq, nq, heads):
    row = lax.broadcasted_iota(jnp.int32, (tq, tq), 0)
    col = lax.broadcasted_iota(jnp.int32, (tq, tq), 1)
    keep = row <= col
    units = [(qi, hh) for qi in reversed(range(nq)) for hh in range(heads)]
    stats = [dict() for _ in units]

    def tiles(qi):
        below = [slice(lo, min(lo + MLA_KEY_TILE, qi * tq))
                 for lo in range(0, qi * tq, MLA_KEY_TILE)]
        return below + [slice(qi * tq, (qi + 1) * tq)]

    def scores(u):
        qi, hh = units[u]
        lanes = slice(hh * HEAD_PAD, (hh + 1) * HEAD_PAD)
        qh = qt_ref[lanes, qi * tq:(qi + 1) * tq]
        m8 = None
        for krows in tiles(qi):
            s = jnp.dot(k_ref[krows, lanes], qh, preferred_element_type=F32)
            if krows.start == qi * tq:
                s = jnp.where(keep, s, -jnp.inf)
            s_ref[u % MLA_BUFS, krows, :] = s
            tile_max = _fold_rows(s, jnp.max)
            m8 = tile_max if m8 is None else jnp.maximum(m8, tile_max)
            yield
        stats[u]["m"] = jnp.max(m8, axis=0, keepdims=True)

    def probs(u):
        qi, _ = units[u]
        m = stats[u]["m"]
        l8 = jnp.zeros((SUBLANES, tq), F32)
        for krows in tiles(qi):
            p = jnp.exp2(s_ref[u % MLA_BUFS, krows, :] - m)
            l8 = l8 + _fold_rows(p, jnp.sum)
            p_ref[u % MLA_BUFS, krows, :] = p.astype(BF16)
            yield
        stats[u]["l"] = jnp.sum(l8, axis=0, keepdims=True)

    def values(u):
        qi, hh = units[u]
        kv_len = (qi + 1) * tq
        vrows = slice(hh * MLA_VDIM, (hh + 1) * MLA_VDIM)
        o_t = jnp.dot(vt_ref[vrows, 0:kv_len], p_ref[u % MLA_BUFS, 0:kv_len, :],
                      preferred_element_type=F32)
        o_ref[vrows, qi * tq:(qi + 1) * tq] = (o_t / stats[u]["l"]).astype(BF16)
        yield

    stages = (scores, probs, values)
    for step in range(len(units) + len(stages) - 1):
        live = [stage(step - d) for d, stage in enumerate(stages)
                if 0 <= step - d < len(units)]
        while live:
            live = [g for g in live if next(g, StopIteration) is not StopIteration]


def _mla_attn(qt, k2, vt, B, S, tq):
    heads = MLA_HEADS_PER_STEP
    return pl.pallas_call(
        functools.partial(_mla_body, tq=tq, nq=S // tq, heads=heads),
        out_shape=jax.ShapeDtypeStruct((MLA_WIDTH, B * S), BF16),
        grid=(B, MLA_HEADS // heads),
        in_specs=[pl.BlockSpec((heads * HEAD_PAD, S), lambda b, p: (p, b)),
                  pl.BlockSpec((S, heads * HEAD_PAD), lambda b, p: (b, p)),
                  pl.BlockSpec((heads * MLA_VDIM, S), lambda b, p: (p, b))],
        out_specs=pl.BlockSpec((heads * MLA_VDIM, S), lambda b, p: (p, b)),
        scratch_shapes=[pltpu.VMEM((MLA_BUFS, S, tq), F32),
                        pltpu.VMEM((MLA_BUFS, S, tq), BF16)],
        compiler_params=pltpu.CompilerParams(
            dimension_semantics=("arbitrary", "arbitrary"), vmem_limit_bytes=VMEM_LIMIT),
        name="mla_attn",
    )(qt, k2, vt)


GLA_BLOCK = 256
GLA_HEADS_PER_STEP = 2


def _gla_body(q_ref, k_ref, v_ref, la_ref, g_ref, o_ref, oi_ref, qin_ref, u_ref, dec_ref,
              *, seq, heads):
    C, BLK = GLA_CHUNK, GLA_BLOCK
    per_blk = BLK // C
    n_chunks = seq // C
    r = lax.broadcasted_iota(jnp.int32, (BLK, BLK), 0)
    c = lax.broadcasted_iota(jnp.int32, (BLK, BLK), 1)
    causal = (c <= r) & (r // C == c // C)
    tril = jnp.where(causal, 1.0, 0.0).astype(BF16)
    gain = g_ref[...]

    chunk_of_row = lax.broadcasted_iota(jnp.int32, (BLK, GLA_HK), 0) // C
    units = [(blk, hh) for blk in range(seq // BLK) for hh in range(heads)]
    vals = [dict() for _ in units]
    state = [jnp.zeros((GLA_HV, GLA_HK), F32) for _ in range(heads)]

    def kcols(hh):
        return slice(hh * GLA_HK, (hh + 1) * GLA_HK)

    def vcols(hh):
        return slice(hh * GLA_HV, (hh + 1) * GLA_HV)

    def decay(u):
        blk, hh = units[u]
        g = la_ref[blk * BLK:(blk + 1) * BLK, kcols(hh)]
        g_hi = g.astype(BF16)
        g_lo = (g - g_hi.astype(F32)).astype(BF16)
        bb = jnp.dot(tril, jnp.concatenate([g_hi, g_lo], axis=1), preferred_element_type=F32)
        vals[u]["b"] = bb[:, :GLA_HK] + bb[:, GLA_HK:]
        yield

    def scale(u):
        blk, hh = units[u]
        rows = slice(blk * BLK, (blk + 1) * BLK)
        b = vals[u].pop("b")
        lasts = [b[(j + 1) * C - 1:(j + 1) * C, :] for j in range(per_blk)]
        b_last = jnp.concatenate([jnp.broadcast_to(t, (C, GLA_HK)) for t in lasts], axis=0)
        q = q_ref[rows, kcols(hh)]
        k = k_ref[rows, kcols(hh)]
        q_in = (q * jnp.exp(b)).astype(BF16)
        k_in = (k * jnp.exp(-b)).astype(BF16)
        k_st = (k * jnp.exp(b_last - b)).astype(BF16)
        yield
        qin_ref[rows, kcols(hh)] = q_in
        for j in range(per_blk):
            dec_ref[hh * n_chunks + blk * per_blk + j] = jnp.exp(lasts[j])
        zero = jnp.zeros_like(k_st)
        vals[u]["k_bd"] = jnp.concatenate(
            [jnp.where(chunk_of_row == j, k_st, zero) for j in range(per_blk)], axis=1)
        vals[u]["attn"] = lax.dot_general(q_in, k_in, _NT, preferred_element_type=F32)
        yield

    def mix(u):
        blk, hh = units[u]
        rows = slice(blk * BLK, (blk + 1) * BLK)
        attn = jnp.where(causal, vals[u].pop("attn"), 0.0).astype(BF16)
        v = v_ref[rows, vcols(hh)]
        oi_ref[rows, vcols(hh)] = jnp.dot(attn, v, preferred_element_type=F32)
        yield
        u_all = lax.dot_general(v, vals[u].pop("k_bd"), _TN,
                                preferred_element_type=F32)
        for j in range(per_blk):
            u_ref[hh * n_chunks + blk * per_blk + j] = u_all[:, j * GLA_HK:(j + 1) * GLA_HK]
        yield

    def scan(u):
        blk, hh = units[u]
        for n in range(blk * per_blk, (blk + 1) * per_blk):
            crow = slice(n * C, (n + 1) * C)
            st = state[hh]
            o = oi_ref[crow, vcols(hh)] + lax.dot_general(
                qin_ref[crow, kcols(hh)], st.astype(BF16), _NT, preferred_element_type=F32)
            o_ref[crow, vcols(hh)] = _rms(o, gain).astype(BF16)
            state[hh] = st * dec_ref[hh * n_chunks + n] + u_ref[hh * n_chunks + n]
            yield

    stages = (decay, scale, mix, scan)
    for step in range(len(units) + len(stages) - 1):
        live = [stage(step - d) for d, stage in enumerate(stages)
                if 0 <= step - d < len(units)]
        while live:
            live = [g for g in live if next(g, StopIteration) is not StopIteration]


def _gla(q3, k3, v3, la3, gain):
    B, S, _ = q3.shape
    heads = GLA_HEADS_PER_STEP
    n_chunks = S // GLA_CHUNK

    def spec(w):
        return pl.BlockSpec((None, S, heads * w), lambda b, h: (b, 0, h))

    return pl.pallas_call(
        functools.partial(_gla_body, seq=S, heads=heads),
        out_shape=jax.ShapeDtypeStruct((B, S, GLA_DV), BF16),
        grid=(B, GLA_HEADS // heads),
        in_specs=[spec(GLA_HK), spec(GLA_HK), spec(GLA_HV), spec(GLA_HK),
                  _const_spec(gain.shape)],
        out_specs=spec(GLA_HV),
        scratch_shapes=[pltpu.VMEM((S, heads * GLA_HV), F32),
                        pltpu.VMEM((S, heads * GLA_HK), BF16),
                        pltpu.VMEM((heads * n_chunks, GLA_HV, GLA_HK), F32),
                        pltpu.VMEM((heads * n_chunks, 1, GLA_HK), F32)],
        compiler_params=pltpu.CompilerParams(
            dimension_semantics=("arbitrary", "arbitrary"), vmem_limit_bytes=VMEM_LIMIT),
        name="gla",
    )(q3, k3, v3, la3, gain)


def _out_body(x_ref, omt_ref, szmt_ref, og_ref, szg_ref, gm_ref, gg_ref,
              wpm_ref, wpg_ref, wo_ref, gf_ref, o_ref, *, final_norm):
    umt = (omt_ref[...].astype(F32) * szmt_ref[...].astype(F32)).astype(BF16)
    ug = (og_ref[...].astype(F32) * szg_ref[...].astype(F32)).astype(BF16)
    y_mla = lax.dot_general(umt, wpm_ref[...], _TN, preferred_element_type=F32)
    y_gla = jnp.dot(ug, wpg_ref[...], preferred_element_type=F32)
    merged = gm_ref[...].astype(F32) * y_mla + gg_ref[...].astype(F32) * y_gla
    r = x_ref[...] + jnp.dot(merged.astype(BF16), wo_ref[...], preferred_element_type=F32)
    o_ref[...] = _rms(r, gf_ref[...]) if final_norm else r


def _out_proj(x2, omt, szmt, og, szg, gm, gg, wpm, wpg, wo, gf, tm, final_norm):
    T = x2.shape[0]

    def row(w):
        return pl.BlockSpec((tm, w), lambda i: (i, 0))

    def colmajor(r):
        return pl.BlockSpec((r, tm), lambda i: (0, i))

    return pl.pallas_call(
        functools.partial(_out_body, final_norm=final_norm),
        out_shape=jax.ShapeDtypeStruct((T, D_MODEL), F32),
        grid=(T // tm,),
        in_specs=[row(D_MODEL), colmajor(MLA_WIDTH), colmajor(MLA_WIDTH), row(GLA_DV),
                  row(GLA_DV), row(D_MODEL), row(D_MODEL), _const_spec(wpm.shape),
                  _const_spec(wpg.shape), _const_spec(wo.shape), _const_spec(gf.shape)],
        out_specs=row(D_MODEL),
        compiler_params=pltpu.CompilerParams(
            dimension_semantics=("arbitrary",), vmem_limit_bytes=VMEM_LIMIT),
        name="out_proj",
    )(x2, omt, szmt, og, szg, gm, gg, wpm, wpg, wo, gf)


def _pad_heads(w, width):
    r = w.shape[0]
    w = w.reshape(r, MLA_HEADS, width)
    w = jnp.pad(w, ((0, 0), (0, 0), (0, HEAD_PAD - width)))
    return w.reshape(r, QPAD)


def kernel(x, positions, g_in, w_in, g_q, w_uq, g_kv, w_ukv, w_gla_gate, b_gla_gate,
           g_gla, w_proj_mla, w_proj_gla, w_out, g_final):
    B, S, D = x.shape
    T = B * S
    depth = w_in.shape[0]
    tm = 512

    freq = (ROPE_THETA ** (-jnp.arange(ROPE_HALF, dtype=F32) / ROPE_HALF)).reshape(ROPE_HALF, 1)
    pos = positions.astype(F32).reshape(1, T)

    x2 = x.reshape(T, D)
    for l in range(depth):
        wt = w_in[l].T.astype(BF16)
        lo = _ROWS["alr"][0]
        wlr = jnp.concatenate(
            [wt[:LOW_RANK], wt[lo:lo + GLA_GATE_RANK],
             jnp.zeros((LANES - GLA_GATE_RANK, D), BF16)], axis=0)
        wuqt = _pad_heads(w_uq[l] * (MLA_QK ** -0.5 * LOG2E), MLA_QK).T.astype(BF16)
        wukv = w_ukv[l].reshape(MLA_KV_RANK, MLA_HEADS, MLA_NOPE + MLA_VDIM)
        wuk = _pad_heads(wukv[:, :, :MLA_NOPE].reshape(MLA_KV_RANK, -1), MLA_NOPE).astype(BF16)
        wuvt = wukv[:, :, MLA_NOPE:].reshape(MLA_KV_RANK, MLA_WIDTH).T.astype(BF16)
        wgg = jnp.pad(w_gla_gate[l], ((0, LANES - GLA_GATE_RANK), (0, 0))).astype(BF16)

        (qt, k, vt, szmt, qg, kg, vg, la, szg, gm, gg) = _in_proj(
            x2, pos, freq, g_in[l].reshape(1, D), wt, wlr, g_q[l].reshape(1, -1), wuqt,
            g_kv[l].reshape(1, -1), wuk, wuvt, wgg, b_gla_gate[l].reshape(1, -1), tm)

        omt = _mla_attn(qt, k, vt, B, S, tq=256)
        o_gla = _gla(qg.reshape(B, S, GLA_DK), kg.reshape(B, S, GLA_DK),
                     vg.reshape(B, S, GLA_DV), la.reshape(B, S, GLA_DK),
                     g_gla[l].reshape(1, GLA_HV))

        x2 = _out_proj(x2, omt, szmt, o_gla.reshape(T, GLA_DV), szg,
                       gm, gg, w_proj_mla[l].astype(BF16), w_proj_gla[l].astype(BF16),
                       w_out[l].astype(BF16), g_final.reshape(1, D), tm,
                       final_norm=(l == depth - 1))
    return x2.reshape(B, S, D)
```

```python
import functools

import jax
import jax.numpy as jnp
from jax import lax
from jax.experimental import pallas as pl
from jax.experimental.pallas import tpu as pltpu

D_MODEL = 1024
EPS = 1e-6
MLA_HEADS = 8
MLA_NOPE = 64
MLA_ROPE = 32
MLA_VDIM = 64
MLA_Q_RANK = 384
MLA_KV_RANK = 256
MLA_QK = MLA_NOPE + MLA_ROPE
MLA_WIDTH = MLA_HEADS * MLA_VDIM
ROPE_THETA = 10000.0
GLA_HEADS = 4
GLA_DK = D_MODEL // 2
GLA_DV = D_MODEL
GLA_HK = GLA_DK // GLA_HEADS
GLA_HV = GLA_DV // GLA_HEADS
GLA_GATE_RANK = 16
GLA_GATE_NORM = 16.0
GLA_CHUNK = 64
SPLITS = (MLA_Q_RANK, MLA_KV_RANK, MLA_ROPE, MLA_WIDTH,
          GLA_DK, GLA_DK, GLA_DV, GLA_GATE_RANK, GLA_DV,
          D_MODEL, D_MODEL)

LANES = 128
SUBLANES = 8
HEAD_PAD = LANES
ROPE_HALF = MLA_ROPE // 2
ROPE_LO = MLA_NOPE
QPAD = MLA_HEADS * HEAD_PAD
LOW_RANK = MLA_Q_RANK + MLA_KV_RANK
LOG2E = 1.4426950408889634

_ROWS = {}
_o = 0
for _n, _w in zip(("cq", "ckv", "kr", "zmla", "qg", "kg", "vg", "alr", "zgla", "gmla", "ggla"),
                  SPLITS):
    _ROWS[_n] = (_o, _o + _w)
    _o += _w

VMEM_LIMIT = 52 * 1024 * 1024

BF16 = jnp.bfloat16
F32 = jnp.float32
_NT = (((1,), (1,)), ((), ()))
_TN = (((0,), (0,)), ((), ()))


def _const_spec(shape):
    nd = len(shape)
    return pl.BlockSpec(shape, lambda *_: (0,) * nd, pipeline_mode=pl.Buffered(1))


def _rms(v, g):
    return v * lax.rsqrt(jnp.mean(v * v, axis=-1, keepdims=True) + EPS) * g


def _in_proj_body(x_ref, pos_ref, freq_ref, gin_ref, wt_ref, wlr_ref, gq_ref, wuqt_ref,
                  gkv_ref, wuk_ref, wuvt_ref, wgg_ref, bgg_ref,
                  qt_ref, k_ref, vt_ref, szmt_ref, qg_ref, kg_ref, vg_ref, la_ref,
                  szg_ref, gm_ref, gg_ref):
    h = _rms(x_ref[...], gin_ref[...]).astype(BF16)
    tm = h.shape[0]

    def tok(name):
        lo, hi = _ROWS[name]
        return lax.dot_general(h, wt_ref[lo:hi, :], _NT, preferred_element_type=F32)

    def sigmoid_to(ref, z, gate=False):
        sg = 0.5 * jnp.tanh(0.5 * z) + 0.5
        ref[...] = (z * sg if gate else sg).astype(BF16)

    low = lax.dot_general(h, wlr_ref[...], _NT, preferred_element_type=F32)
    cq, ckv, alr = (low[:, :MLA_Q_RANK], low[:, MLA_Q_RANK:LOW_RANK],
                    low[:, LOW_RANK:].astype(BF16))
    qn = _rms(cq, gq_ref[...]).astype(BF16)
    lo = _ROWS["kr"][0]
    krz = lax.dot_general(wt_ref[lo:_ROWS["zmla"][1], :], h, _NT,
                          preferred_element_type=F32)
    kvn = _rms(ckv, gkv_ref[...]).astype(BF16)
    qt = lax.dot_general(wuqt_ref[...], qn, _NT, preferred_element_type=F32)

    ang = freq_ref[...] * pos_ref[...]
    cos, sin = jnp.cos(ang), jnp.sin(ang)

    def rope(x1, x2):
        return x1 * cos - x2 * sin, x2 * cos + x1 * sin

    zg = jnp.dot(alr, wgg_ref[...], preferred_element_type=F32) + bgg_ref[...]
    gm = tok("gmla")
    log_sig = jnp.minimum(zg, 0.0) - jnp.log(1.0 + jnp.exp(-jnp.abs(zg)))
    la_ref[...] = log_sig * (1.0 / GLA_GATE_NORM)
    for hd in range(MLA_HEADS):
        base = hd * HEAD_PAD
        lo = base + ROPE_LO
        r1, r2 = rope(qt[lo:lo + ROPE_HALF], qt[lo + ROPE_HALF:lo + MLA_ROPE])
        head = jnp.concatenate([qt[base:lo], r1, r2, qt[lo + MLA_ROPE:base + HEAD_PAD]], axis=0)
        qt_ref[base:base + HEAD_PAD, :] = head.astype(BF16)

    kn = jnp.dot(kvn, wuk_ref[...], preferred_element_type=F32)
    sigmoid_to(gm_ref, gm)
    gg = tok("ggla")
    r1, r2 = rope(krz[:ROPE_HALF], krz[ROPE_HALF:MLA_ROPE])
    kr = jnp.concatenate([jnp.zeros((ROPE_LO, tm), F32), r1, r2,
                          jnp.zeros((HEAD_PAD - ROPE_LO - MLA_ROPE, tm), F32)], axis=0).T
    for hd in range(MLA_HEADS):
        sl = slice(hd * HEAD_PAD, (hd + 1) * HEAD_PAD)
        k_ref[:, sl] = (kn[:, sl] + kr).astype(BF16)

    vt = lax.dot_general(wuvt_ref[...], kvn, _NT, preferred_element_type=F32)
    sigmoid_to(gg_ref, gg)
    zg_full = tok("zgla")
    vt_ref[...] = vt.astype(BF16)
    sigmoid_to(szmt_ref, krz[MLA_ROPE:], gate=True)
    qg = tok("qg")
    sigmoid_to(szg_ref, zg_full, gate=True)
    kg = tok("kg")
    qg_ref[...] = qg * (GLA_HK ** -0.5)
    vg = tok("vg")
    kg_ref[...] = kg
    vg_ref[...] = vg.astype(BF16)


def _in_proj(x2, pos, freq, gin, wt, wlr, gq, wuqt, gkv, wuk, wuvt, wgg, bgg, tm):
    T = x2.shape[0]

    def row(w):
        return pl.BlockSpec((tm, w), lambda i: (i, 0))

    def colmajor(r):
        return pl.BlockSpec((r, tm), lambda i: (0, i))

    sds = jax.ShapeDtypeStruct
    out_shape = (sds((QPAD, T), BF16), sds((T, QPAD), BF16),
                 sds((MLA_WIDTH, T), BF16), sds((MLA_WIDTH, T), BF16),
                 sds((T, GLA_DK), F32), sds((T, GLA_DK), F32), sds((T, GLA_DV), BF16),
                 sds((T, GLA_DK), F32), sds((T, GLA_DV), BF16),
                 sds((T, D_MODEL), BF16), sds((T, D_MODEL), BF16))
    out_specs = (colmajor(QPAD), row(QPAD), colmajor(MLA_WIDTH), colmajor(MLA_WIDTH),
                 row(GLA_DK), row(GLA_DK), row(GLA_DV), row(GLA_DK), row(GLA_DV),
                 row(D_MODEL), row(D_MODEL))
    consts = (freq, gin, wt, wlr, gq, wuqt, gkv, wuk, wuvt, wgg, bgg)
    return pl.pallas_call(
        _in_proj_body,
        out_shape=out_shape,
        grid=(T // tm,),
        in_specs=[row(D_MODEL), colmajor(1)] + [_const_spec(c.shape) for c in consts],
        out_specs=out_specs,
        compiler_params=pltpu.CompilerParams(
            dimension_semantics=("arbitrary",), vmem_limit_bytes=VMEM_LIMIT),
        name="in_proj",
    )(x2, pos, *consts)


MLA_BUFS = 4
MLA_KEY_TILE = 256
MLA_HEADS_PER_STEP = 4


def _fold_rows(t, op):
    return op(t.reshape(t.shape[0] // SUBLANES, SUBLANES, t.shape[1]), axis=0)


def _mla_body(qt_ref, k_ref, vt_ref, o_ref, s_ref, p_ref, *, tq, nq, heads):
    row = lax.broadcasted_iota(jnp.int32, (tq, tq), 0)
    col = lax.broadcasted_iota(jnp.int32, (tq, tq), 1)
    keep = row <= col
    units = [(qi, hh) for qi in reversed(range(nq)) for hh in range(heads)]
    stats = [dict() for _ in units]

    def tiles(qi):
        below = [slice(lo, min(lo + MLA_KEY_TILE, qi * tq))
                 for lo in range(0, qi * tq, MLA_KEY_TILE)]
        return below + [slice(qi * tq, (qi + 1) * tq)]

    def scores(u):
        qi, hh = units[u]
        lanes = slice(hh * HEAD_PAD, (hh + 1) * HEAD_PAD)
        qh = qt_ref[lanes, qi * tq:(qi + 1) * tq]
        m8 = None
        for krows in tiles(qi):
            s = jnp.dot(k_ref[krows, lanes], qh, preferred_element_type=F32)
            if krows.start == qi * tq:
                s = jnp.where(keep, s, -jnp.inf)
            s_ref[u % MLA_BUFS, krows, :] = s
            tile_max = _fold_rows(s, jnp.max)
            m8 = tile_max if m8 is None else jnp.maximum(m8, tile_max)
            yield
        stats[u]["m"] = jnp.max(m8, axis=0, keepdims=True)

    def probs(u):
        qi, _ = units[u]
        m = stats[u]["m"]
        l8 = jnp.zeros((SUBLANES, tq), F32)
        for krows in tiles(qi):
            p = jnp.exp2(s_ref[u % MLA_BUFS, krows, :] - m)
            l8 = l8 + _fold_rows(p, jnp.sum)
            p_ref[u % MLA_BUFS, krows, :] = p.astype(BF16)
            yield
        stats[u]["l"] = jnp.sum(l8, axis=0, keepdims=True)

    def values(u):
        qi, hh = units[u]
        kv_len = (qi + 1) * tq
        vrows = slice(hh * MLA_VDIM, (hh + 1) * MLA_VDIM)
        o_t = jnp.dot(vt_ref[vrows, 0:kv_len], p_ref[u % MLA_BUFS, 0:kv_len, :],
                      preferred_element_type=F32)
        o_ref[vrows, qi * tq:(qi + 1) * tq] = (o_t / stats[u]["l"]).astype(BF16)
        yield

    stages = (scores, probs, values)
    for step in range(len(units) + len(stages) - 1):
        live = [stage(step - d) for d, stage in enumerate(stages)
                if 0 <= step - d < len(units)]
        while live:
            live = [g for g in live if next(g, StopIteration) is not StopIteration]


def _mla_attn(qt, k2, vt, B, S, tq):
    heads = MLA_HEADS_PER_STEP
    return pl.pallas_call(
        functools.partial(_mla_body, tq=tq, nq=S // tq, heads=heads),
        out_shape=jax.ShapeDtypeStruct((MLA_WIDTH, B * S), BF16),
        grid=(B, MLA_HEADS // heads),
        in_specs=[pl.BlockSpec((heads * HEAD_PAD, S), lambda b, p: (p, b)),
                  pl.BlockSpec((S, heads * HEAD_PAD), lambda b, p: (b, p)),
                  pl.BlockSpec((heads * MLA_VDIM, S), lambda b, p: (p, b))],
        out_specs=pl.BlockSpec((heads * MLA_VDIM, S), lambda b, p: (p, b)),
        scratch_shapes=[pltpu.VMEM((MLA_BUFS, S, tq), F32),
                        pltpu.VMEM((MLA_BUFS, S, tq), BF16)],
        compiler_params=pltpu.CompilerParams(
            dimension_semantics=("arbitrary", "arbitrary"), vmem_limit_bytes=VMEM_LIMIT),
        name="mla_attn",
    )(qt, k2, vt)


GLA_BLOCK = 256
GLA_HEADS_PER_STEP = 2


def _gla_body(q_ref, k_ref, v_ref, la_ref, g_ref, o_ref, oi_ref, qin_ref, u_ref, dec_ref,
              *, seq, heads):
    C, BLK = GLA_CHUNK, GLA_BLOCK
    per_blk = BLK // C
    n_chunks = seq // C
    r = lax.broadcasted_iota(jnp.int32, (BLK, BLK), 0)
    c = lax.broadcasted_iota(jnp.int32, (BLK, BLK), 1)
    causal = (c <= r) & (r // C == c // C)
    tril = jnp.where(causal, 1.0, 0.0).astype(BF16)
    gain = g_ref[...]

    chunk_of_row = lax.broadcasted_iota(jnp.int32, (BLK, GLA_HK), 0) // C
    units = [(blk, hh) for blk in range(seq // BLK) for hh in range(heads)]
    vals = [dict() for _ in units]
    state = [jnp.zeros((GLA_HV, GLA_HK), F32) for _ in range(heads)]

    def kcols(hh):
        return slice(hh * GLA_HK, (hh + 1) * GLA_HK)

    def vcols(hh):
        return slice(hh * GLA_HV, (hh + 1) * GLA_HV)

    def decay(u):
        blk, hh = units[u]
        g = la_ref[blk * BLK:(blk + 1) * BLK, kcols(hh)]
        g_hi = g.astype(BF16)
        g_lo = (g - g_hi.astype(F32)).astype(BF16)
        bb = jnp.dot(tril, jnp.concatenate([g_hi, g_lo], axis=1), preferred_element_type=F32)
        vals[u]["b"] = bb[:, :GLA_HK] + bb[:, GLA_HK:]
        yield

    def scale(u):
        blk, hh = units[u]
        rows = slice(blk * BLK, (blk + 1) * BLK)
        b = vals[u].pop("b")
        lasts = [b[(j + 1) * C - 1:(j + 1) * C, :] for j in range(per_blk)]
        b_last = jnp.concatenate([jnp.broadcast_to(t, (C, GLA_HK)) for t in lasts], axis=0)
        q = q_ref[rows, kcols(hh)]
        k = k_ref[rows, kcols(hh)]
        q_in = (q * jnp.exp(b)).astype(BF16)
        k_in = (k * jnp.exp(-b)).astype(BF16)
        k_st = (k * jnp.exp(b_last - b)).astype(BF16)
        yield
        qin_ref[rows, kcols(hh)] = q_in
        for j in range(per_blk):
            dec_ref[hh * n_chunks + blk * per_blk + j] = jnp.exp(lasts[j])
        zero = jnp.zeros_like(k_st)
        vals[u]["k_bd"] = jnp.concatenate(
            [jnp.where(chunk_of_row == j, k_st, zero) for j in range(per_blk)], axis=1)
        vals[u]["attn"] = lax.dot_general(q_in, k_in, _NT, preferred_element_type=F32)
        yield

    def mix(u):
        blk, hh = units[u]
        rows = slice(blk * BLK, (blk + 1) * BLK)
        attn = jnp.where(causal, vals[u].pop("attn"), 0.0).astype(BF16)
        v = v_ref[rows, vcols(hh)]
        oi_ref[rows, vcols(hh)] = jnp.dot(attn, v, preferred_element_type=F32)
        yield
        u_all = lax.dot_general(v, vals[u].pop("k_bd"), _TN,
                                preferred_element_type=F32)
        for j in range(per_blk):
            u_ref[hh * n_chunks + blk * per_blk + j] = u_all[:, j * GLA_HK:(j + 1) * GLA_HK]
        yield

    def scan(u):
        blk, hh = units[u]
        for n in range(blk * per_blk, (blk + 1) * per_blk):
            crow = slice(n * C, (n + 1) * C)
            st = state[hh]
            o = oi_ref[crow, vcols(hh)] + lax.dot_general(
                qin_ref[crow, kcols(hh)], st.astype(BF16), _NT, preferred_element_type=F32)
            o_ref[crow, vcols(hh)] = _rms(o, gain).astype(BF16)
            state[hh] = st * dec_ref[hh * n_chunks + n] + u_ref[hh * n_chunks + n]
            yield

    stages = (decay, scale, mix, scan)
    for step in range(len(units) + len(stages) - 1):
        live = [stage(step - d) for d, stage in enumerate(stages)
                if 0 <= step - d < len(units)]
        while live:
            live = [g for g in live if next(g, StopIteration) is not StopIteration]


def _gla(q3, k3, v3, la3, gain):
    B, S, _ = q3.shape
    heads = GLA_HEADS_PER_STEP
    n_chunks = S // GLA_CHUNK

    def spec(w):
        return pl.BlockSpec((None, S, heads * w), lambda b, h: (b, 0, h))

    return pl.pallas_call(
        functools.partial(_gla_body, seq=S, heads=heads),
        out_shape=jax.ShapeDtypeStruct((B, S, GLA_DV), BF16),
        grid=(B, GLA_HEADS // heads),
        in_specs=[spec(GLA_HK), spec(GLA_HK), spec(GLA_HV), spec(GLA_HK),
                  _const_spec(gain.shape)],
        out_specs=spec(GLA_HV),
        scratch_shapes=[pltpu.VMEM((S, heads * GLA_HV), F32),
                        pltpu.VMEM((S, heads * GLA_HK), BF16),
                        pltpu.VMEM((heads * n_chunks, GLA_HV, GLA_HK), F32),
                        pltpu.VMEM((heads * n_chunks, 1, GLA_HK), F32)],
        compiler_params=pltpu.CompilerParams(
            dimension_semantics=("arbitrary", "arbitrary"), vmem_limit_bytes=VMEM_LIMIT),
        name="gla",
    )(q3, k3, v3, la3, gain)


def _out_body(x_ref, omt_ref, szmt_ref, og_ref, szg_ref, gm_ref, gg_ref,
              wpm_ref, wpg_ref, wo_ref, gf_ref, o_ref, *, final_norm):
    umt = (omt_ref[...].astype(F32) * szmt_ref[...].astype(F32)).astype(BF16)
    ug = (og_ref[...].astype(F32) * szg_ref[...].astype(F32)).astype(BF16)
    y_mla = lax.dot_general(umt, wpm_ref[...], _TN, preferred_element_type=F32)
    y_gla = jnp.dot(ug, wpg_ref[...], preferred_element_type=F32)
    merged = gm_ref[...].astype(F32) * y_mla + gg_ref[...].astype(F32) * y_gla
    r = x_ref[...] + jnp.dot(merged.astype(BF16), wo_ref[...], preferred_element_type=F32)
    o_ref[...] = _rms(r, gf_ref[...]) if final_norm else r


def _out_proj(x2, omt, szmt, og, szg, gm, gg, wpm, wpg, wo, gf, tm, final_norm):
    T = x2.shape[0]

    def row(w):
        return pl.BlockSpec((tm, w), lambda i: (i, 0))

    def colmajor(r):
        return pl.BlockSpec((r, tm), lambda i: (0, i))

    return pl.pallas_call(
        functools.partial(_out_body, final_norm=final_norm),
        out_shape=jax.ShapeDtypeStruct((T, D_MODEL), F32),
        grid=(T // tm,),
        in_specs=[row(D_MODEL), colmajor(MLA_WIDTH), colmajor(MLA_WIDTH), row(GLA_DV),
                  row(GLA_DV), row(D_MODEL), row(D_MODEL), _const_spec(wpm.shape),
                  _const_spec(wpg.shape), _const_spec(wo.shape), _const_spec(gf.shape)],
        out_specs=row(D_MODEL),
        compiler_params=pltpu.CompilerParams(
            dimension_semantics=("arbitrary",), vmem_limit_bytes=VMEM_LIMIT),
        name="out_proj",
    )(x2, omt, szmt, og, szg, gm, gg, wpm, wpg, wo, gf)


def _pad_heads(w, width):
    r = w.shape[0]
    w = w.reshape(r, MLA_HEADS, width)
    w = jnp.pad(w, ((0, 0), (0, 0), (0, HEAD_PAD - width)))
    return w.reshape(r, QPAD)


def kernel(x, positions, g_in, w_in, g_q, w_uq, g_kv, w_ukv, w_gla_gate, b_gla_gate,
           g_gla, w_proj_mla, w_proj_gla, w_out, g_final):
    B, S, D = x.shape
    T = B * S
    depth = w_in.shape[0]
    tm = 512

    freq = (ROPE_THETA ** (-jnp.arange(ROPE_HALF, dtype=F32) / ROPE_HALF)).reshape(ROPE_HALF, 1)
    pos = positions.astype(F32).reshape(1, T)

    x2 = x.reshape(T, D)
    for l in range(depth):
        wt = w_in[l].T.astype(BF16)
        lo = _ROWS["alr"][0]
        wlr = jnp.concatenate(
            [wt[:LOW_RANK], wt[lo:lo + GLA_GATE_RANK],
             jnp.zeros((LANES - GLA_GATE_RANK, D), BF16)], axis=0)
        wuqt = _pad_heads(w_uq[l] * (MLA_QK ** -0.5 * LOG2E), MLA_QK).T.astype(BF16)
        wukv = w_ukv[l].reshape(MLA_KV_RANK, MLA_HEADS, MLA_NOPE + MLA_VDIM)
        wuk = _pad_heads(wukv[:, :, :MLA_NOPE].reshape(MLA_KV_RANK, -1), MLA_NOPE).astype(BF16)
        wuvt = wukv[:, :, MLA_NOPE:].reshape(MLA_KV_RANK, MLA_WIDTH).T.astype(BF16)
        wgg = jnp.pad(w_gla_gate[l], ((0, LANES - GLA_GATE_RANK), (0, 0))).astype(BF16)

        (qt, k, vt, szmt, qg, kg, vg, la, szg, gm, gg) = _in_proj(
            x2, pos, freq, g_in[l].reshape(1, D), wt, wlr, g_q[l].reshape(1, -1), wuqt,
            g_kv[l].reshape(1, -1), wuk, wuvt, wgg, b_gla_gate[l].reshape(1, -1), tm)

        omt = _mla_attn(qt, k, vt, B, S, tq=256)
        o_gla = _gla(qg.reshape(B, S, GLA_DK), kg.reshape(B, S, GLA_DK),
                     vg.reshape(B, S, GLA_DV), la.reshape(B, S, GLA_DK),
                     g_gla[l].reshape(1, GLA_HV))

        x2 = _out_proj(x2, omt, szmt, o_gla.reshape(T, GLA_DV), szg,
                       gm, gg, w_proj_mla[l].astype(BF16), w_proj_gla[l].astype(BF16),
                       w_out[l].astype(BF16), g_final.reshape(1, D), 2 * tm,
                       final_norm=(l == depth - 1))
    return x2.reshape(B, S, D)
```

```python
import functools

import jax
import jax.numpy as jnp
from jax import lax
from jax.experimental import pallas as pl
from jax.experimental.pallas import tpu as pltpu

D_MODEL = 1024
EPS = 1e-6
MLA_HEADS = 8
MLA_NOPE = 64
MLA_ROPE = 32
MLA_VDIM = 64
MLA_Q_RANK = 384
MLA_KV_RANK = 256
MLA_QK = MLA_NOPE + MLA_ROPE
MLA_WIDTH = MLA_HEADS * MLA_VDIM
ROPE_THETA = 10000.0
GLA_HEADS = 4
GLA_DK = D_MODEL // 2
GLA_DV = D_MODEL
GLA_HK = GLA_DK // GLA_HEADS
GLA_HV = GLA_DV // GLA_HEADS
GLA_GATE_RANK = 16
GLA_GATE_NORM = 16.0
GLA_CHUNK = 64
SPLITS = (MLA_Q_RANK, MLA_KV_RANK, MLA_ROPE, MLA_WIDTH,
          GLA_DK, GLA_DK, GLA_DV, GLA_GATE_RANK, GLA_DV,
          D_MODEL, D_MODEL)

LANES = 128
SUBLANES = 8
HEAD_PAD = LANES
ROPE_HALF = MLA_ROPE // 2
ROPE_LO = MLA_NOPE
QPAD = MLA_HEADS * HEAD_PAD
LOW_RANK = MLA_Q_RANK + MLA_KV_RANK
LOG2E = 1.4426950408889634

_ROWS = {}
_o = 0
for _n, _w in zip(("cq", "ckv", "kr", "zmla", "qg", "kg", "vg", "alr", "zgla", "gmla", "ggla"),
                  SPLITS):
    _ROWS[_n] = (_o, _o + _w)
    _o += _w

VMEM_LIMIT = 52 * 1024 * 1024
IN_TOKEN_TILE = 512
OUT_TOKEN_TILE = 1024
MLA_QUERY_BLOCK = 256

BF16 = jnp.bfloat16
F32 = jnp.float32
_NT = (((1,), (1,)), ((), ()))
_TN = (((0,), (0,)), ((), ()))


def _const_spec(shape):
    nd = len(shape)
    return pl.BlockSpec(shape, lambda *_: (0,) * nd, pipeline_mode=pl.Buffered(1))


def _unit_rms(v):
    return v * lax.rsqrt(jnp.mean(v * v, axis=-1, keepdims=True) + EPS)


def _rms(v, g):
    return _unit_rms(v) * g


def _in_proj_body(x_ref, pos_ref, freq_ref, wt_ref, wlr_ref, wuqt_ref,
                  wuk_ref, wuvt_ref, wgg_ref, bgg_ref,
                  qt_ref, k_ref, vt_ref, szmt_ref, qg_ref, kg_ref, vg_ref, la_ref,
                  szg_ref, gm_ref, gg_ref):
    h = _unit_rms(x_ref[...]).astype(BF16)
    tm = h.shape[0]

    def tok(name):
        lo, hi = _ROWS[name]
        return lax.dot_general(h, wt_ref[lo:hi, :], _NT, preferred_element_type=F32)

    def sigmoid_to(ref, z, gate=False):
        sg = 0.5 * jnp.tanh(0.5 * z) + 0.5
        ref[...] = (z * sg if gate else sg).astype(BF16)

    low = lax.dot_general(h, wlr_ref[...], _NT, preferred_element_type=F32)
    cq, ckv, alr = (low[:, :MLA_Q_RANK], low[:, MLA_Q_RANK:LOW_RANK],
                    low[:, LOW_RANK:].astype(BF16))
    qn = _unit_rms(cq).astype(BF16)
    lo = _ROWS["kr"][0]
    krz = lax.dot_general(wt_ref[lo:_ROWS["zmla"][1], :], h, _NT,
                          preferred_element_type=F32)
    kvn = _unit_rms(ckv).astype(BF16)
    qt = lax.dot_general(wuqt_ref[...], qn, _NT, preferred_element_type=F32)

    ang = freq_ref[...] * pos_ref[...]
    cos, sin = jnp.cos(ang), jnp.sin(ang)

    def rope(x1, x2):
        return x1 * cos - x2 * sin, x2 * cos + x1 * sin

    zg = jnp.dot(alr, wgg_ref[...], preferred_element_type=F32) + bgg_ref[...]
    gm = tok("gmla")
    log_sig = jnp.minimum(zg, 0.0) - jnp.log(1.0 + jnp.exp(-jnp.abs(zg)))
    la_ref[...] = log_sig * (1.0 / GLA_GATE_NORM)
    for hd in range(MLA_HEADS):
        base = hd * HEAD_PAD
        lo = base + ROPE_LO
        r1, r2 = rope(qt[lo:lo + ROPE_HALF], qt[lo + ROPE_HALF:lo + MLA_ROPE])
        head = jnp.concatenate([qt[base:lo], r1, r2, qt[lo + MLA_ROPE:base + HEAD_PAD]], axis=0)
        qt_ref[base:base + HEAD_PAD, :] = head.astype(BF16)

    kn = jnp.dot(kvn, wuk_ref[...], preferred_element_type=F32)
    sigmoid_to(gm_ref, gm)
    gg = tok("ggla")
    r1, r2 = rope(krz[:ROPE_HALF], krz[ROPE_HALF:MLA_ROPE])
    kr = jnp.concatenate([jnp.zeros((ROPE_LO, tm), F32), r1, r2,
                          jnp.zeros((HEAD_PAD - ROPE_LO - MLA_ROPE, tm), F32)], axis=0).T
    for hd in range(MLA_HEADS):
        sl = slice(hd * HEAD_PAD, (hd + 1) * HEAD_PAD)
        k_ref[:, sl] = (kn[:, sl] + kr).astype(BF16)

    vt = lax.dot_general(wuvt_ref[...], kvn, _NT, preferred_element_type=F32)
    sigmoid_to(gg_ref, gg)
    zg_full = tok("zgla")
    vt_ref[...] = vt.astype(BF16)
    sigmoid_to(szmt_ref, krz[MLA_ROPE:], gate=True)
    qg = tok("qg")
    sigmoid_to(szg_ref, zg_full, gate=True)
    kg = tok("kg")
    qg_ref[...] = qg * (GLA_HK ** -0.5)
    vg = tok("vg")
    kg_ref[...] = kg
    vg_ref[...] = vg.astype(BF16)


def _in_proj(x2, pos, freq, wt, wlr, wuqt, wuk, wuvt, wgg, bgg, tm):
    T = x2.shape[0]

    def row(w):
        return pl.BlockSpec((tm, w), lambda i: (i, 0))

    def colmajor(r):
        return pl.BlockSpec((r, tm), lambda i: (0, i))

    sds = jax.ShapeDtypeStruct
    out_shape = (sds((QPAD, T), BF16), sds((T, QPAD), BF16),
                 sds((MLA_WIDTH, T), BF16), sds((MLA_WIDTH, T), BF16),
                 sds((T, GLA_DK), F32), sds((T, GLA_DK), F32), sds((T, GLA_DV), BF16),
                 sds((T, GLA_DK), F32), sds((T, GLA_DV), BF16),
                 sds((T, D_MODEL), BF16), sds((T, D_MODEL), BF16))
    out_specs = (colmajor(QPAD), row(QPAD), colmajor(MLA_WIDTH), colmajor(MLA_WIDTH),
                 row(GLA_DK), row(GLA_DK), row(GLA_DV), row(GLA_DK), row(GLA_DV),
                 row(D_MODEL), row(D_MODEL))
    consts = (freq, wt, wlr, wuqt, wuk, wuvt, wgg, bgg)
    return pl.pallas_call(
        _in_proj_body,
        out_shape=out_shape,
        grid=(T // tm,),
        in_specs=[row(D_MODEL), colmajor(1)] + [_const_spec(c.shape) for c in consts],
        out_specs=out_specs,
        compiler_params=pltpu.CompilerParams(
            dimension_semantics=("arbitrary",), vmem_limit_bytes=VMEM_LIMIT),
        name="in_proj",
    )(x2, pos, *consts)


MLA_BUFS = 4
MLA_KEY_TILE = 256
MLA_HEADS_PER_STEP = 4


def _fold_rows(t, op):
    return op(t.reshape(t.shape[0] // SUBLANES, SUBLANES, t.shape[1]), axis=0)


def _mla_body(qt_ref, k_ref, vt_ref, o_ref, s_ref, p_ref, *, tq, nq, heads):
    row = lax.broadcasted_iota(jnp.int32, (tq, tq), 0)
    col = lax.broadcasted_iota(jnp.int32, (tq, tq), 1)
    keep = row <= col
    units = [(qi, hh) for qi in reversed(range(nq)) for hh in range(heads)]
    stats = [dict() for _ in units]

    def tiles(qi):
        below = [slice(lo, min(lo + MLA_KEY_TILE, qi * tq))
                 for lo in range(0, qi * tq, MLA_KEY_TILE)]
        return below + [slice(qi * tq, (qi + 1) * tq)]

    def scores(u):
        qi, hh = units[u]
        lanes = slice(hh * HEAD_PAD, (hh + 1) * HEAD_PAD)
        qh = qt_ref[lanes, qi * tq:(qi + 1) * tq]
        m8 = None
        for krows in tiles(qi):
            s = jnp.dot(k_ref[krows, lanes], qh, preferred_element_type=F32)
            if krows.start == qi * tq:
                s = jnp.where(keep, s, -jnp.inf)
            s_ref[u % MLA_BUFS, krows, :] = s
            tile_max = _fold_rows(s, jnp.max)
            m8 = tile_max if m8 is None else jnp.maximum(m8, tile_max)
            yield
        stats[u]["m"] = jnp.max(m8, axis=0, keepdims=True)

    def probs(u):
        qi, _ = units[u]
        m = stats[u]["m"]
        l8 = jnp.zeros((SUBLANES, tq), F32)
        for krows in tiles(qi):
            p = jnp.exp2(s_ref[u % MLA_BUFS, krows, :] - m)
            l8 = l8 + _fold_rows(p, jnp.sum)
            p_ref[u % MLA_BUFS, krows, :] = p.astype(BF16)
            yield
        stats[u]["l"] = jnp.sum(l8, axis=0, keepdims=True)

    def values(u):
        qi, hh = units[u]
        kv_len = (qi + 1) * tq
        vrows = slice(hh * MLA_VDIM, (hh + 1) * MLA_VDIM)
        o_t = jnp.dot(vt_ref[vrows, 0:kv_len], p_ref[u % MLA_BUFS, 0:kv_len, :],
                      preferred_element_type=F32)
        o_ref[vrows, qi * tq:(qi + 1) * tq] = (o_t / stats[u]["l"]).astype(BF16)
        yield

    stages = (scores, probs, values)
    for step in range(len(units) + len(stages) - 1):
        live = [stage(step - d) for d, stage in enumerate(stages)
                if 0 <= step - d < len(units)]
        while live:
            live = [g for g in live if next(g, StopIteration) is not StopIteration]


def _mla_attn(qt, k2, vt, B, S, tq):
    heads = MLA_HEADS_PER_STEP
    return pl.pallas_call(
        functools.partial(_mla_body, tq=tq, nq=S // tq, heads=heads),
        out_shape=jax.ShapeDtypeStruct((MLA_WIDTH, B * S), BF16),
        grid=(B, MLA_HEADS // heads),
        in_specs=[pl.BlockSpec((heads * HEAD_PAD, S), lambda b, p: (p, b)),
                  pl.BlockSpec((S, heads * HEAD_PAD), lambda b, p: (b, p)),
                  pl.BlockSpec((heads * MLA_VDIM, S), lambda b, p: (p, b))],
        out_specs=pl.BlockSpec((heads * MLA_VDIM, S), lambda b, p: (p, b)),
        scratch_shapes=[pltpu.VMEM((MLA_BUFS, S, tq), F32),
                        pltpu.VMEM((MLA_BUFS, S, tq), BF16)],
        compiler_params=pltpu.CompilerParams(
            dimension_semantics=("arbitrary", "arbitrary"), vmem_limit_bytes=VMEM_LIMIT),
        name="mla_attn",
    )(qt, k2, vt)


GLA_BLOCK = 256
GLA_HEADS_PER_STEP = 2


def _gla_body(q_ref, k_ref, v_ref, la_ref, o_ref, oi_ref, qin_ref, u_ref, dec_ref,
              *, seq, heads):
    C, BLK = GLA_CHUNK, GLA_BLOCK
    per_blk = BLK // C
    n_chunks = seq // C
    r = lax.broadcasted_iota(jnp.int32, (BLK, BLK), 0)
    c = lax.broadcasted_iota(jnp.int32, (BLK, BLK), 1)
    causal = (c <= r) & (r // C == c // C)
    tril = jnp.where(causal, 1.0, 0.0).astype(BF16)

    chunk_of_row = lax.broadcasted_iota(jnp.int32, (BLK, GLA_HK), 0) // C
    units = [(blk, hh) for blk in range(seq // BLK) for hh in range(heads)]
    vals = [dict() for _ in units]
    state = [jnp.zeros((GLA_HV, GLA_HK), F32) for _ in range(heads)]

    def kcols(hh):
        return slice(hh * GLA_HK, (hh + 1) * GLA_HK)

    def vcols(hh):
        return slice(hh * GLA_HV, (hh + 1) * GLA_HV)

    def decay(u):
        blk, hh = units[u]
        g = la_ref[blk * BLK:(blk + 1) * BLK, kcols(hh)]
        g_hi = g.astype(BF16)
        g_lo = (g - g_hi.astype(F32)).astype(BF16)
        bb = jnp.dot(tril, jnp.concatenate([g_hi, g_lo], axis=1), preferred_element_type=F32)
        vals[u]["b"] = bb[:, :GLA_HK] + bb[:, GLA_HK:]
        yield

    def scale(u):
        blk, hh = units[u]
        rows = slice(blk * BLK, (blk + 1) * BLK)
        b = vals[u].pop("b")
        lasts = [b[(j + 1) * C - 1:(j + 1) * C, :] for j in range(per_blk)]
        b_last = jnp.concatenate([jnp.broadcast_to(t, (C, GLA_HK)) for t in lasts], axis=0)
        q = q_ref[rows, kcols(hh)]
        k = k_ref[rows, kcols(hh)]
        q_in = (q * jnp.exp(b)).astype(BF16)
        k_in = (k * jnp.exp(-b)).astype(BF16)
        k_st = (k * jnp.exp(b_last - b)).astype(BF16)
        yield
        qin_ref[rows, kcols(hh)] = q_in
        for j in range(per_blk):
            dec_ref[hh * n_chunks + blk * per_blk + j] = jnp.exp(lasts[j])
        zero = jnp.zeros_like(k_st)
        vals[u]["k_bd"] = jnp.concatenate(
            [jnp.where(chunk_of_row == j, k_st, zero) for j in range(per_blk)], axis=1)
        vals[u]["attn"] = lax.dot_general(q_in, k_in, _NT, preferred_element_type=F32)
        yield

    def mix(u):
        blk, hh = units[u]
        rows = slice(blk * BLK, (blk + 1) * BLK)
        attn = jnp.where(causal, vals[u].pop("attn"), 0.0).astype(BF16)
        v = v_ref[rows, vcols(hh)]
        oi_ref[rows, vcols(hh)] = jnp.dot(attn, v, preferred_element_type=F32)
        yield
        u_all = lax.dot_general(v, vals[u].pop("k_bd"), _TN,
                                preferred_element_type=F32)
        for j in range(per_blk):
            u_ref[hh * n_chunks + blk * per_blk + j] = u_all[:, j * GLA_HK:(j + 1) * GLA_HK]
        yield

    def scan(u):
        blk, hh = units[u]
        for n in range(blk * per_blk, (blk + 1) * per_blk):
            crow = slice(n * C, (n + 1) * C)
            st = state[hh]
            o = oi_ref[crow, vcols(hh)] + lax.dot_general(
                qin_ref[crow, kcols(hh)], st.astype(BF16), _NT, preferred_element_type=F32)
            o_ref[crow, vcols(hh)] = _unit_rms(o).astype(BF16)
            state[hh] = st * dec_ref[hh * n_chunks + n] + u_ref[hh * n_chunks + n]
            yield

    stages = (decay, scale, mix, scan)
    for step in range(len(units) + len(stages) - 1):
        live = [stage(step - d) for d, stage in enumerate(stages)
                if 0 <= step - d < len(units)]
        while live:
            live = [g for g in live if next(g, StopIteration) is not StopIteration]


def _gla(q3, k3, v3, la3):
    B, S, _ = q3.shape
    heads = GLA_HEADS_PER_STEP
    n_chunks = S // GLA_CHUNK

    def spec(w):
        return pl.BlockSpec((None, S, heads * w), lambda b, h: (b, 0, h))

    return pl.pallas_call(
        functools.partial(_gla_body, seq=S, heads=heads),
        out_shape=jax.ShapeDtypeStruct((B, S, GLA_DV), BF16),
        grid=(B, GLA_HEADS // heads),
        in_specs=[spec(GLA_HK), spec(GLA_HK), spec(GLA_HV), spec(GLA_HK)],
        out_specs=spec(GLA_HV),
        scratch_shapes=[pltpu.VMEM((S, heads * GLA_HV), F32),
                        pltpu.VMEM((S, heads * GLA_HK), BF16),
                        pltpu.VMEM((heads * n_chunks, GLA_HV, GLA_HK), F32),
                        pltpu.VMEM((heads * n_chunks, 1, GLA_HK), F32)],
        compiler_params=pltpu.CompilerParams(
            dimension_semantics=("arbitrary", "arbitrary"), vmem_limit_bytes=VMEM_LIMIT),
        name="gla",
    )(q3, k3, v3, la3)


def _out_body(x_ref, omt_ref, szmt_ref, og_ref, szg_ref, gm_ref, gg_ref,
              wpm_ref, wpg_ref, wo_ref, gf_ref, o_ref, *, final_norm):
    umt = (omt_ref[...].astype(F32) * szmt_ref[...].astype(F32)).astype(BF16)
    ug = (og_ref[...].astype(F32) * szg_ref[...].astype(F32)).astype(BF16)
    y_mla = lax.dot_general(umt, wpm_ref[...], _TN, preferred_element_type=F32)
    y_gla = jnp.dot(ug, wpg_ref[...], preferred_element_type=F32)
    merged = gm_ref[...].astype(F32) * y_mla + gg_ref[...].astype(F32) * y_gla
    r = x_ref[...] + jnp.dot(merged.astype(BF16), wo_ref[...], preferred_element_type=F32)
    o_ref[...] = _rms(r, gf_ref[...]) if final_norm else r


def _out_proj(x2, omt, szmt, og, szg, gm, gg, wpm, wpg, wo, gf, tm, final_norm):
    T = x2.shape[0]

    def row(w):
        return pl.BlockSpec((tm, w), lambda i: (i, 0))

    def colmajor(r):
        return pl.BlockSpec((r, tm), lambda i: (0, i))

    return pl.pallas_call(
        functools.partial(_out_body, final_norm=final_norm),
        out_shape=jax.ShapeDtypeStruct((T, D_MODEL), F32),
        grid=(T // tm,),
        in_specs=[row(D_MODEL), colmajor(MLA_WIDTH), colmajor(MLA_WIDTH), row(GLA_DV),
                  row(GLA_DV), row(D_MODEL), row(D_MODEL), _const_spec(wpm.shape),
                  _const_spec(wpg.shape), _const_spec(wo.shape), _const_spec(gf.shape)],
        out_specs=row(D_MODEL),
        compiler_params=pltpu.CompilerParams(
            dimension_semantics=("arbitrary",), vmem_limit_bytes=VMEM_LIMIT),
        name="out_proj",
    )(x2, omt, szmt, og, szg, gm, gg, wpm, wpg, wo, gf)


def _pad_heads(w, width):
    r = w.shape[0]
    w = w.reshape(r, MLA_HEADS, width)
    w = jnp.pad(w, ((0, 0), (0, 0), (0, HEAD_PAD - width)))
    return w.reshape(r, QPAD)


def kernel(x, positions, g_in, w_in, g_q, w_uq, g_kv, w_ukv, w_gla_gate, b_gla_gate,
           g_gla, w_proj_mla, w_proj_gla, w_out, g_final):
    B, S, D = x.shape
    T = B * S
    depth = w_in.shape[0]
    assert D == D_MODEL and w_in.shape[1:] == (D_MODEL, sum(SPLITS))
    assert T % OUT_TOKEN_TILE == 0 and OUT_TOKEN_TILE % IN_TOKEN_TILE == 0
    assert S % MLA_QUERY_BLOCK == 0 and S % GLA_BLOCK == 0

    freq = (ROPE_THETA ** (-jnp.arange(ROPE_HALF, dtype=F32) / ROPE_HALF)).reshape(ROPE_HALF, 1)
    pos = positions.astype(F32).reshape(1, T)

    x2 = x.reshape(T, D)
    for l in range(depth):
        wt = (w_in[l] * g_in[l][:, None]).T.astype(BF16)
        lo = _ROWS["alr"][0]
        wlr = jnp.concatenate(
            [wt[:LOW_RANK], wt[lo:lo + GLA_GATE_RANK],
             jnp.zeros((LANES - GLA_GATE_RANK, D), BF16)], axis=0)
        wuqt = _pad_heads(w_uq[l] * g_q[l][:, None] * (MLA_QK ** -0.5 * LOG2E),
                          MLA_QK).T.astype(BF16)
        wukv = (w_ukv[l] * g_kv[l][:, None]).reshape(
            MLA_KV_RANK, MLA_HEADS, MLA_NOPE + MLA_VDIM)
        wuk = _pad_heads(wukv[:, :, :MLA_NOPE].reshape(MLA_KV_RANK, -1), MLA_NOPE).astype(BF16)
        wuvt = wukv[:, :, MLA_NOPE:].reshape(MLA_KV_RANK, MLA_WIDTH).T.astype(BF16)
        wgg = jnp.pad(w_gla_gate[l], ((0, LANES - GLA_GATE_RANK), (0, 0))).astype(BF16)
        wpg = (w_proj_gla[l] * jnp.tile(g_gla[l], GLA_HEADS)[:, None]).astype(BF16)

        (qt, k, vt, szmt, qg, kg, vg, la, szg, gm, gg) = _in_proj(
            x2, pos, freq, wt, wlr, wuqt, wuk, wuvt, wgg, b_gla_gate[l].reshape(1, -1),
            IN_TOKEN_TILE)

        omt = _mla_attn(qt, k, vt, B, S, tq=MLA_QUERY_BLOCK)
        o_gla = _gla(qg.reshape(B, S, GLA_DK), kg.reshape(B, S, GLA_DK),
                     vg.reshape(B, S, GLA_DV), la.reshape(B, S, GLA_DK))

        x2 = _out_proj(x2, omt, szmt, o_gla.reshape(T, GLA_DV), szg,
                       gm, gg, w_proj_mla[l].astype(BF16), wpg,
                       w_out[l].astype(BF16), g_final.reshape(1, D), OUT_TOKEN_TILE,
                       final_norm=(l == depth - 1))
    return x2.reshape(B, S, D)
```

```python
import functools

import jax
import jax.numpy as jnp
from jax import lax
from jax.experimental import pallas as pl
from jax.experimental.pallas import tpu as pltpu

D_MODEL = 1024
EPS = 1e-6
MLA_HEADS = 8
MLA_NOPE = 64
MLA_ROPE = 32
MLA_VDIM = 64
MLA_Q_RANK = 384
MLA_KV_RANK = 256
MLA_QK = MLA_NOPE + MLA_ROPE
MLA_WIDTH = MLA_HEADS * MLA_VDIM
ROPE_THETA = 10000.0
GLA_HEADS = 4
GLA_DK = D_MODEL // 2
GLA_DV = D_MODEL
GLA_HK = GLA_DK // GLA_HEADS
GLA_HV = GLA_DV // GLA_HEADS
GLA_GATE_RANK = 16
GLA_GATE_NORM = 16.0
GLA_CHUNK = 64
SPLITS = (MLA_Q_RANK, MLA_KV_RANK, MLA_ROPE, MLA_WIDTH,
          GLA_DK, GLA_DK, GLA_DV, GLA_GATE_RANK, GLA_DV,
          D_MODEL, D_MODEL)

LANES = 128
SUBLANES = 8
HEAD_PAD = LANES
ROPE_HALF = MLA_ROPE // 2
ROPE_LO = MLA_NOPE
QPAD = MLA_HEADS * HEAD_PAD
LOW_RANK = MLA_Q_RANK + MLA_KV_RANK
LOG2E = 1.4426950408889634

_ROWS = {}
_o = 0
for _n, _w in zip(("cq", "ckv", "kr", "zmla", "qg", "kg", "vg", "alr", "zgla", "gmla", "ggla"),
                  SPLITS):
    _ROWS[_n] = (_o, _o + _w)
    _o += _w

VMEM_LIMIT = 52 * 1024 * 1024
IN_TOKEN_TILE = 512
OUT_TOKEN_TILE = 1024
MLA_QUERY_BLOCK = 256

BF16 = jnp.bfloat16
F32 = jnp.float32
_NT = (((1,), (1,)), ((), ()))
_TN = (((0,), (0,)), ((), ()))


def _const_spec(shape):
    nd = len(shape)
    return pl.BlockSpec(shape, lambda *_: (0,) * nd, pipeline_mode=pl.Buffered(1))


def _unit_rms(v):
    return v * lax.rsqrt(jnp.mean(v * v, axis=-1, keepdims=True) + EPS)


def _rms(v, g):
    return _unit_rms(v) * g


def _in_proj_body(x_ref, pos_ref, freq_ref, wt_ref, wlr_ref, wuqt_ref,
                  wuk_ref, wuvt_ref, wgg_ref, bgg_ref,
                  qt_ref, k_ref, vt_ref, szmt_ref, qg_ref, kg_ref, vg_ref, la_ref,
                  szg_ref, gm_ref, gg_ref):
    h = _unit_rms(x_ref[...]).astype(BF16)
    tm = h.shape[0]

    def tok(name):
        lo, hi = _ROWS[name]
        return lax.dot_general(h, wt_ref[lo:hi, :], _NT, preferred_element_type=F32)

    def sigmoid_to(ref, z, gate=False):
        sg = 0.5 * jnp.tanh(0.5 * z) + 0.5
        ref[...] = (z * sg if gate else sg).astype(BF16)

    low = lax.dot_general(h, wlr_ref[...], _NT, preferred_element_type=F32)
    cq, ckv, alr = (low[:, :MLA_Q_RANK], low[:, MLA_Q_RANK:LOW_RANK],
                    low[:, LOW_RANK:].astype(BF16))
    qn = _unit_rms(cq).astype(BF16)
    lo = _ROWS["kr"][0]
    krz = lax.dot_general(wt_ref[lo:_ROWS["zmla"][1], :], h, _NT,
                          preferred_element_type=F32)
    kvn = _unit_rms(ckv).astype(BF16)
    qt = lax.dot_general(wuqt_ref[...], qn, _NT, preferred_element_type=F32)

    ang = freq_ref[...] * pos_ref[...]
    cos, sin = jnp.cos(ang), jnp.sin(ang)

    def rope(x1, x2):
        return x1 * cos - x2 * sin, x2 * cos + x1 * sin

    zg = jnp.dot(alr, wgg_ref[...], preferred_element_type=F32) + bgg_ref[...]
    gm = tok("gmla")
    log_sig = jnp.minimum(zg, 0.0) - jnp.log(1.0 + jnp.exp(-jnp.abs(zg)))
    la_ref[...] = log_sig * (1.0 / GLA_GATE_NORM)
    for hd in range(MLA_HEADS):
        base = hd * HEAD_PAD
        lo = base + ROPE_LO
        r1, r2 = rope(qt[lo:lo + ROPE_HALF], qt[lo + ROPE_HALF:lo + MLA_ROPE])
        head = jnp.concatenate([qt[base:lo], r1, r2, qt[lo + MLA_ROPE:base + HEAD_PAD]], axis=0)
        qt_ref[base:base + HEAD_PAD, :] = head.astype(BF16)

    kn = jnp.dot(kvn, wuk_ref[...], preferred_element_type=F32)
    sigmoid_to(gm_ref, gm)
    gg = tok("ggla")
    r1, r2 = rope(krz[:ROPE_HALF], krz[ROPE_HALF:MLA_ROPE])
    kr = jnp.concatenate([jnp.zeros((ROPE_LO, tm), F32), r1, r2,
                          jnp.zeros((HEAD_PAD - ROPE_LO - MLA_ROPE, tm), F32)], axis=0).T
    for hd in range(MLA_HEADS):
        sl = slice(hd * HEAD_PAD, (hd + 1) * HEAD_PAD)
        k_ref[:, sl] = (kn[:, sl] + kr).astype(BF16)

    vt = lax.dot_general(wuvt_ref[...], kvn, _NT, preferred_element_type=F32)
    sigmoid_to(gg_ref, gg)
    zg_full = tok("zgla")
    vt_ref[...] = vt.astype(BF16)
    sigmoid_to(szmt_ref, krz[MLA_ROPE:], gate=True)
    qg = tok("qg")
    sigmoid_to(szg_ref, zg_full, gate=True)
    kg = tok("kg")
    qg_ref[...] = qg * (GLA_HK ** -0.5)
    vg = tok("vg")
    kg_ref[...] = kg
    vg_ref[...] = vg.astype(BF16)


def _in_proj(x2, pos, freq, wt, wlr, wuqt, wuk, wuvt, wgg, bgg, tm):
    T = x2.shape[0]

    def row(w):
        return pl.BlockSpec((tm, w), lambda i: (i, 0))

    def colmajor(r):
        return pl.BlockSpec((r, tm), lambda i: (0, i))

    sds = jax.ShapeDtypeStruct
    out_shape = (sds((QPAD, T), BF16), sds((T, QPAD), BF16),
                 sds((MLA_WIDTH, T), BF16), sds((MLA_WIDTH, T), BF16),
                 sds((T, GLA_DK), F32), sds((T, GLA_DK), F32), sds((T, GLA_DV), BF16),
                 sds((T, GLA_DK), F32), sds((T, GLA_DV), BF16),
                 sds((T, D_MODEL), BF16), sds((T, D_MODEL), BF16))
    out_specs = (colmajor(QPAD), row(QPAD), colmajor(MLA_WIDTH), colmajor(MLA_WIDTH),
                 row(GLA_DK), row(GLA_DK), row(GLA_DV), row(GLA_DK), row(GLA_DV),
                 row(D_MODEL), row(D_MODEL))
    consts = (freq, wt, wlr, wuqt, wuk, wuvt, wgg, bgg)
    return pl.pallas_call(
        _in_proj_body,
        out_shape=out_shape,
        grid=(T // tm,),
        in_specs=[row(D_MODEL), colmajor(1)] + [_const_spec(c.shape) for c in consts],
        out_specs=out_specs,
        compiler_params=pltpu.CompilerParams(
            dimension_semantics=("arbitrary",), vmem_limit_bytes=VMEM_LIMIT),
        name="in_proj",
    )(x2, pos, *consts)


MLA_BUFS = 4
MLA_KEY_TILE = 256
MLA_HEADS_PER_STEP = 4


def _fold_rows(t, op):
    return op(t.reshape(t.shape[0] // SUBLANES, SUBLANES, t.shape[1]), axis=0)


def _mla_body(qt_ref, k_ref, vt_ref, o_ref, s_ref, p_ref, *, tq, nq, heads):
    row = lax.broadcasted_iota(jnp.int32, (tq, tq), 0)
    col = lax.broadcasted_iota(jnp.int32, (tq, tq), 1)
    keep = row <= col
    units = [(qi, hh) for qi in reversed(range(nq)) for hh in range(heads)]
    stats = [dict() for _ in units]

    def tiles(qi):
        below = [slice(lo, min(lo + MLA_KEY_TILE, qi * tq))
                 for lo in range(0, qi * tq, MLA_KEY_TILE)]
        return below + [slice(qi * tq, (qi + 1) * tq)]

    def scores(u):
        qi, hh = units[u]
        lanes = slice(hh * HEAD_PAD, (hh + 1) * HEAD_PAD)
        qh = qt_ref[lanes, qi * tq:(qi + 1) * tq]
        m8 = None
        for krows in tiles(qi):
            s = jnp.dot(k_ref[krows, lanes], qh, preferred_element_type=F32)
            if krows.start == qi * tq:
                s = jnp.where(keep, s, -jnp.inf)
            s_ref[u % MLA_BUFS, krows, :] = s
            tile_max = _fold_rows(s, jnp.max)
            m8 = tile_max if m8 is None else jnp.maximum(m8, tile_max)
            yield
        stats[u]["m"] = jnp.max(m8, axis=0, keepdims=True)

    def probs(u):
        qi, _ = units[u]
        m = stats[u]["m"]
        l8 = jnp.zeros((SUBLANES, tq), F32)
        for krows in tiles(qi):
            p = jnp.exp2(s_ref[u % MLA_BUFS, krows, :] - m)
            l8 = l8 + _fold_rows(p, jnp.sum)
            p_ref[u % MLA_BUFS, krows, :] = p.astype(BF16)
            yield
        stats[u]["l"] = jnp.sum(l8, axis=0, keepdims=True)

    def values(u):
        qi, hh = units[u]
        kv_len = (qi + 1) * tq
        vrows = slice(hh * MLA_VDIM, (hh + 1) * MLA_VDIM)
        o_t = jnp.dot(vt_ref[vrows, 0:kv_len], p_ref[u % MLA_BUFS, 0:kv_len, :],
                      preferred_element_type=F32)
        o_ref[vrows, qi * tq:(qi + 1) * tq] = (o_t / stats[u]["l"]).astype(BF16)
        yield

    stages = (scores, probs, values)
    for step in range(len(units) + len(stages) - 1):
        live = [stage(step - d) for d, stage in enumerate(stages)
                if 0 <= step - d < len(units)]
        while live:
            live = [g for g in live if next(g, StopIteration) is not StopIteration]


def _mla_attn(qt, k2, vt, B, S, tq):
    heads = MLA_HEADS_PER_STEP
    return pl.pallas_call(
        functools.partial(_mla_body, tq=tq, nq=S // tq, heads=heads),
        out_shape=jax.ShapeDtypeStruct((MLA_WIDTH, B * S), BF16),
        grid=(B, MLA_HEADS // heads),
        in_specs=[pl.BlockSpec((heads * HEAD_PAD, S), lambda b, p: (p, b)),
                  pl.BlockSpec((S, heads * HEAD_PAD), lambda b, p: (b, p)),
                  pl.BlockSpec((heads * MLA_VDIM, S), lambda b, p: (p, b))],
        out_specs=pl.BlockSpec((heads * MLA_VDIM, S), lambda b, p: (p, b)),
        scratch_shapes=[pltpu.VMEM((MLA_BUFS, S, tq), F32),
                        pltpu.VMEM((MLA_BUFS, S, tq), BF16)],
        compiler_params=pltpu.CompilerParams(
            dimension_semantics=("arbitrary", "arbitrary"), vmem_limit_bytes=VMEM_LIMIT),
        name="mla_attn",
    )(qt, k2, vt)


GLA_BLOCK = 256
GLA_HEADS_PER_STEP = 2


def _gla_body(q_ref, k_ref, v_ref, la_ref, o_ref, oi_ref, qin_ref, u_ref, dec_ref,
              *, seq, heads):
    C, BLK = GLA_CHUNK, GLA_BLOCK
    per_blk = BLK // C
    n_chunks = seq // C
    r = lax.broadcasted_iota(jnp.int32, (BLK, BLK), 0)
    c = lax.broadcasted_iota(jnp.int32, (BLK, BLK), 1)
    causal = (c <= r) & (r // C == c // C)
    tril = jnp.where(causal, 1.0, 0.0).astype(BF16)

    chunk_of_row = lax.broadcasted_iota(jnp.int32, (BLK, GLA_HK), 0) // C
    units = [(blk, hh) for blk in range(seq // BLK) for hh in range(heads)]
    vals = [dict() for _ in units]
    state = [jnp.zeros((GLA_HV, GLA_HK), F32) for _ in range(heads)]

    def kcols(hh):
        return slice(hh * GLA_HK, (hh + 1) * GLA_HK)

    def vcols(hh):
        return slice(hh * GLA_HV, (hh + 1) * GLA_HV)

    def decay(u):
        blk, hh = units[u]
        g = la_ref[blk * BLK:(blk + 1) * BLK, kcols(hh)]
        g_hi = g.astype(BF16)
        g_lo = (g - g_hi.astype(F32)).astype(BF16)
        bb = jnp.dot(tril, jnp.concatenate([g_hi, g_lo], axis=1), preferred_element_type=F32)
        vals[u]["b"] = bb[:, :GLA_HK] + bb[:, GLA_HK:]
        yield

    def scale(u):
        blk, hh = units[u]
        rows = slice(blk * BLK, (blk + 1) * BLK)
        b = vals[u].pop("b")
        lasts = [b[(j + 1) * C - 1:(j + 1) * C, :] for j in range(per_blk)]
        b_last = jnp.concatenate([jnp.broadcast_to(t, (C, GLA_HK)) for t in lasts], axis=0)
        q = q_ref[rows, kcols(hh)]
        k = k_ref[rows, kcols(hh)]
        q_in = (q * jnp.exp(b)).astype(BF16)
        k_in = (k * jnp.exp(-b)).astype(BF16)
        k_st = (k * jnp.exp(b_last - b)).astype(BF16)
        yield
        qin_ref[rows, kcols(hh)] = q_in
        for j in range(per_blk):
            dec_ref[hh * n_chunks + blk * per_blk + j] = jnp.exp(lasts[j])
        zero = jnp.zeros_like(k_st)
        vals[u]["k_bd"] = jnp.concatenate(
            [jnp.where(chunk_of_row == j, k_st, zero) for j in range(per_blk)], axis=1)
        vals[u]["attn"] = lax.dot_general(q_in, k_in, _NT, preferred_element_type=F32)
        yield

    def mix(u):
        blk, hh = units[u]
        rows = slice(blk * BLK, (blk + 1) * BLK)
        attn = jnp.where(causal, vals[u].pop("attn"), 0.0).astype(BF16)
        v = v_ref[rows, vcols(hh)]
        oi_ref[rows, vcols(hh)] = jnp.dot(attn, v, preferred_element_type=F32)
        yield
        u_all = lax.dot_general(v, vals[u].pop("k_bd"), _TN,
                                preferred_element_type=F32)
        for j in range(per_blk):
            u_ref[hh * n_chunks + blk * per_blk + j] = u_all[:, j * GLA_HK:(j + 1) * GLA_HK]
        yield

    def scan(u):
        blk, hh = units[u]
        for n in range(blk * per_blk, (blk + 1) * per_blk):
            crow = slice(n * C, (n + 1) * C)
            st = state[hh]
            o = oi_ref[crow, vcols(hh)] + lax.dot_general(
                qin_ref[crow, kcols(hh)], st.astype(BF16), _NT, preferred_element_type=F32)
            o_ref[crow, vcols(hh)] = _unit_rms(o).astype(BF16)
            state[hh] = st * dec_ref[hh * n_chunks + n] + u_ref[hh * n_chunks + n]
            yield

    stages = (decay, scale, mix, scan)
    for step in range(len(units) + len(stages) - 1):
        live = [stage(step - d) for d, stage in enumerate(stages)
                if 0 <= step - d < len(units)]
        while live:
            live = [g for g in live if next(g, StopIteration) is not StopIteration]


def _gla(q3, k3, v3, la3):
    B, S, _ = q3.shape
    heads = GLA_HEADS_PER_STEP
    n_chunks = S // GLA_CHUNK

    def spec(w):
        return pl.BlockSpec((None, S, heads * w), lambda b, h: (b, 0, h))

    return pl.pallas_call(
        functools.partial(_gla_body, seq=S, heads=heads),
        out_shape=jax.ShapeDtypeStruct((B, S, GLA_DV), BF16),
        grid=(B, GLA_HEADS // heads),
        in_specs=[spec(GLA_HK), spec(GLA_HK), spec(GLA_HV), spec(GLA_HK)],
        out_specs=spec(GLA_HV),
        scratch_shapes=[pltpu.VMEM((S, heads * GLA_HV), F32),
                        pltpu.VMEM((S, heads * GLA_HK), BF16),
                        pltpu.VMEM((heads * n_chunks, GLA_HV, GLA_HK), F32),
                        pltpu.VMEM((heads * n_chunks, 1, GLA_HK), F32)],
        compiler_params=pltpu.CompilerParams(
            dimension_semantics=("arbitrary", "arbitrary"), vmem_limit_bytes=VMEM_LIMIT),
        name="gla",
    )(q3, k3, v3, la3)


def _out_body(x_ref, omt_ref, szmt_ref, og_ref, szg_ref, gm_ref, gg_ref,
              wpm_ref, wpg_ref, wo_ref, gf_ref, o_ref, *, final_norm):
    tm = x_ref.shape[0]

    def half(rows):
        umt = (omt_ref[:, rows].astype(F32) * szmt_ref[:, rows].astype(F32)).astype(BF16)
        ug = (og_ref[rows, :].astype(F32) * szg_ref[rows, :].astype(F32)).astype(BF16)
        yield
        y_mla = lax.dot_general(umt, wpm_ref[...], _TN, preferred_element_type=F32)
        y_gla = jnp.dot(ug, wpg_ref[...], preferred_element_type=F32)
        yield
        merged = gm_ref[rows, :].astype(F32) * y_mla + gg_ref[rows, :].astype(F32) * y_gla
        r = x_ref[rows, :] + jnp.dot(merged.astype(BF16), wo_ref[...],
                                     preferred_element_type=F32)
        yield
        o_ref[rows, :] = _rms(r, gf_ref[...]) if final_norm else r
        yield

    a, b = half(slice(0, tm // 2)), half(slice(tm // 2, tm))
    next(a)
    for _ in range(3):
        next(b)
        next(a)
    next(b)


def _out_proj(x2, omt, szmt, og, szg, gm, gg, wpm, wpg, wo, gf, tm, final_norm):
    T = x2.shape[0]

    def row(w):
        return pl.BlockSpec((tm, w), lambda i: (i, 0))

    def colmajor(r):
        return pl.BlockSpec((r, tm), lambda i: (0, i))

    return pl.pallas_call(
        functools.partial(_out_body, final_norm=final_norm),
        out_shape=jax.ShapeDtypeStruct((T, D_MODEL), F32),
        grid=(T // tm,),
        in_specs=[row(D_MODEL), colmajor(MLA_WIDTH), colmajor(MLA_WIDTH), row(GLA_DV),
                  row(GLA_DV), row(D_MODEL), row(D_MODEL), _const_spec(wpm.shape),
                  _const_spec(wpg.shape), _const_spec(wo.shape), _const_spec(gf.shape)],
        out_specs=row(D_MODEL),
        compiler_params=pltpu.CompilerParams(
            dimension_semantics=("arbitrary",), vmem_limit_bytes=VMEM_LIMIT),
        name="out_proj",
    )(x2, omt, szmt, og, szg, gm, gg, wpm, wpg, wo, gf)


def _pad_heads(w, width):
    r = w.shape[0]
    w = w.reshape(r, MLA_HEADS, width)
    w = jnp.pad(w, ((0, 0), (0, 0), (0, HEAD_PAD - width)))
    return w.reshape(r, QPAD)


def kernel(x, positions, g_in, w_in, g_q, w_uq, g_kv, w_ukv, w_gla_gate, b_gla_gate,
           g_gla, w_proj_mla, w_proj_gla, w_out, g_final):
    B, S, D = x.shape
    T = B * S
    depth = w_in.shape[0]
    assert D == D_MODEL and w_in.shape[1:] == (D_MODEL, sum(SPLITS))
    assert T % OUT_TOKEN_TILE == 0 and OUT_TOKEN_TILE % IN_TOKEN_TILE == 0
    assert S % MLA_QUERY_BLOCK == 0 and S % GLA_BLOCK == 0

    freq = (ROPE_THETA ** (-jnp.arange(ROPE_HALF, dtype=F32) / ROPE_HALF)).reshape(ROPE_HALF, 1)
    pos = positions.astype(F32).reshape(1, T)

    x2 = x.reshape(T, D)
    for l in range(depth):
        wt = (w_in[l] * g_in[l][:, None]).T.astype(BF16)
        lo = _ROWS["alr"][0]
        wlr = jnp.concatenate(
            [wt[:LOW_RANK], wt[lo:lo + GLA_GATE_RANK],
             jnp.zeros((LANES - GLA_GATE_RANK, D), BF16)], axis=0)
        wuqt = _pad_heads(w_uq[l] * g_q[l][:, None] * (MLA_QK ** -0.5 * LOG2E),
                          MLA_QK).T.astype(BF16)
        wukv = (w_ukv[l] * g_kv[l][:, None]).reshape(
            MLA_KV_RANK, MLA_HEADS, MLA_NOPE + MLA_VDIM)
        wuk = _pad_heads(wukv[:, :, :MLA_NOPE].reshape(MLA_KV_RANK, -1), MLA_NOPE).astype(BF16)
        wuvt = wukv[:, :, MLA_NOPE:].reshape(MLA_KV_RANK, MLA_WIDTH).T.astype(BF16)
        wgg = jnp.pad(w_gla_gate[l], ((0, LANES - GLA_GATE_RANK), (0, 0))).astype(BF16)
        wpg = (w_proj_gla[l] * jnp.tile(g_gla[l], GLA_HEADS)[:, None]).astype(BF16)

        (qt, k, vt, szmt, qg, kg, vg, la, szg, gm, gg) = _in_proj(
            x2, pos, freq, wt, wlr, wuqt, wuk, wuvt, wgg, b_gla_gate[l].reshape(1, -1),
            IN_TOKEN_TILE)

        omt = _mla_attn(qt, k, vt, B, S, tq=MLA_QUERY_BLOCK)
        o_gla = _gla(qg.reshape(B, S, GLA_DK), kg.reshape(B, S, GLA_DK),
                     vg.reshape(B, S, GLA_DV), la.reshape(B, S, GLA_DK))

        x2 = _out_proj(x2, omt, szmt, o_gla.reshape(T, GLA_DV), szg,
                       gm, gg, w_proj_mla[l].astype(BF16), wpg,
                       w_out[l].astype(BF16), g_final.reshape(1, D), OUT_TOKEN_TILE,
                       final_norm=(l == depth - 1))
    return x2.reshape(B, S, D)
```

```python
import functools

import jax
import jax.numpy as jnp
from jax import lax
from jax.experimental import pallas as pl
from jax.experimental.pallas import tpu as pltpu

D_MODEL = 1024
EPS = 1e-6
MLA_HEADS = 8
MLA_NOPE = 64
MLA_ROPE = 32
MLA_VDIM = 64
MLA_Q_RANK = 384
MLA_KV_RANK = 256
MLA_QK = MLA_NOPE + MLA_ROPE
MLA_WIDTH = MLA_HEADS * MLA_VDIM
ROPE_THETA = 10000.0
GLA_HEADS = 4
GLA_DK = D_MODEL // 2
GLA_DV = D_MODEL
GLA_HK = GLA_DK // GLA_HEADS
GLA_HV = GLA_DV // GLA_HEADS
GLA_GATE_RANK = 16
GLA_GATE_NORM = 16.0
GLA_CHUNK = 64
SPLITS = (MLA_Q_RANK, MLA_KV_RANK, MLA_ROPE, MLA_WIDTH,
          GLA_DK, GLA_DK, GLA_DV, GLA_GATE_RANK, GLA_DV,
          D_MODEL, D_MODEL)

LANES = 128
SUBLANES = 8
HEAD_PAD = LANES
ROPE_HALF = MLA_ROPE // 2
ROPE_LO = MLA_NOPE
QPAD = MLA_HEADS * HEAD_PAD
Q_ROWS = MLA_HEADS * MLA_QK
LOW_RANK = MLA_Q_RANK + MLA_KV_RANK
LOG2E = 1.4426950408889634

_ROWS = {}
_o = 0
for _n, _w in zip(("cq", "ckv", "kr", "zmla", "qg", "kg", "vg", "alr", "zgla", "gmla", "ggla"),
                  SPLITS):
    _ROWS[_n] = (_o, _o + _w)
    _o += _w

VMEM_LIMIT = 52 * 1024 * 1024
IN_TOKEN_TILE = 512
OUT_TOKEN_TILE = 1024
MLA_QUERY_BLOCK = 256

BF16 = jnp.bfloat16
F32 = jnp.float32
_NT = (((1,), (1,)), ((), ()))
_TN = (((0,), (0,)), ((), ()))


def _const_spec(shape):
    nd = len(shape)
    return pl.BlockSpec(shape, lambda *_: (0,) * nd, pipeline_mode=pl.Buffered(1))


def _unit_rms(v):
    return v * lax.rsqrt(jnp.mean(v * v, axis=-1, keepdims=True) + EPS)


def _rms(v, g):
    return _unit_rms(v) * g


def _in_proj_body(x_ref, pos_ref, freq_ref, wt_ref, wlr_ref, wuqt_ref,
                  wuk_ref, wuvt_ref, wgg_ref, bgg_ref,
                  qt_ref, k_ref, vt_ref, szmt_ref, qg_ref, kg_ref, vg_ref, la_ref,
                  szg_ref, gm_ref, gg_ref):
    h = _unit_rms(x_ref[...]).astype(BF16)
    tm = h.shape[0]

    def tok(name):
        lo, hi = _ROWS[name]
        return lax.dot_general(h, wt_ref[lo:hi, :], _NT, preferred_element_type=F32)

    def sigmoid_to(ref, z, gate=False):
        sg = 0.5 * jnp.tanh(0.5 * z) + 0.5
        ref[...] = (z * sg if gate else sg).astype(BF16)

    low = lax.dot_general(h, wlr_ref[...], _NT, preferred_element_type=F32)
    cq, ckv, alr = (low[:, :MLA_Q_RANK], low[:, MLA_Q_RANK:LOW_RANK],
                    low[:, LOW_RANK:].astype(BF16))
    qn = _unit_rms(cq).astype(BF16)
    lo = _ROWS["kr"][0]
    krz = lax.dot_general(wt_ref[lo:_ROWS["zmla"][1], :], h, _NT,
                          preferred_element_type=F32)
    kvn = _unit_rms(ckv).astype(BF16)
    qt = lax.dot_general(wuqt_ref[...], qn, _NT, preferred_element_type=F32)

    ang = freq_ref[...] * pos_ref[...]
    cos, sin = jnp.cos(ang), jnp.sin(ang)

    def rope(x1, x2):
        return x1 * cos - x2 * sin, x2 * cos + x1 * sin

    zg = jnp.dot(alr, wgg_ref[...], preferred_element_type=F32) + bgg_ref[...]
    gm = tok("gmla")
    log_sig = jnp.minimum(zg, 0.0) - jnp.log(1.0 + jnp.exp(-jnp.abs(zg)))
    la_ref[...] = log_sig * (1.0 / GLA_GATE_NORM)
    for hd in range(MLA_HEADS):
        base = hd * MLA_QK
        lo = base + ROPE_LO
        r1, r2 = rope(qt[lo:lo + ROPE_HALF], qt[lo + ROPE_HALF:lo + MLA_ROPE])
        head = jnp.concatenate([qt[base:lo], r1, r2], axis=0)
        qt_ref[base:base + MLA_QK, :] = head.astype(BF16)

    kn = jnp.dot(kvn, wuk_ref[...], preferred_element_type=F32)
    sigmoid_to(gm_ref, gm)
    gg = tok("ggla")
    r1, r2 = rope(krz[:ROPE_HALF], krz[ROPE_HALF:MLA_ROPE])
    kr = jnp.concatenate([jnp.zeros((ROPE_LO, tm), F32), r1, r2,
                          jnp.zeros((HEAD_PAD - ROPE_LO - MLA_ROPE, tm), F32)], axis=0).T
    for hd in range(MLA_HEADS):
        sl = slice(hd * HEAD_PAD, (hd + 1) * HEAD_PAD)
        k_ref[:, sl] = (kn[:, sl] + kr).astype(BF16)

    vt = lax.dot_general(wuvt_ref[...], kvn, _NT, preferred_element_type=F32)
    sigmoid_to(gg_ref, gg)
    zg_full = tok("zgla")
    vt_ref[...] = vt.astype(BF16)
    sigmoid_to(szmt_ref, krz[MLA_ROPE:], gate=True)
    qg = tok("qg")
    sigmoid_to(szg_ref, zg_full, gate=True)
    kg = tok("kg")
    qg_ref[...] = qg * (GLA_HK ** -0.5)
    vg = tok("vg")
    kg_ref[...] = kg
    vg_ref[...] = vg.astype(BF16)


def _in_proj(x2, pos, freq, wt, wlr, wuqt, wuk, wuvt, wgg, bgg, tm):
    T = x2.shape[0]

    def row(w):
        return pl.BlockSpec((tm, w), lambda i: (i, 0))

    def colmajor(r):
        return pl.BlockSpec((r, tm), lambda i: (0, i))

    sds = jax.ShapeDtypeStruct
    out_shape = (sds((Q_ROWS, T), BF16), sds((T, QPAD), BF16),
                 sds((MLA_WIDTH, T), BF16), sds((MLA_WIDTH, T), BF16),
                 sds((T, GLA_DK), F32), sds((T, GLA_DK), F32), sds((T, GLA_DV), BF16),
                 sds((T, GLA_DK), F32), sds((T, GLA_DV), BF16),
                 sds((T, D_MODEL), BF16), sds((T, D_MODEL), BF16))
    out_specs = (colmajor(Q_ROWS), row(QPAD), colmajor(MLA_WIDTH), colmajor(MLA_WIDTH),
                 row(GLA_DK), row(GLA_DK), row(GLA_DV), row(GLA_DK), row(GLA_DV),
                 row(D_MODEL), row(D_MODEL))
    consts = (freq, wt, wlr, wuqt, wuk, wuvt, wgg, bgg)
    return pl.pallas_call(
        _in_proj_body,
        out_shape=out_shape,
        grid=(T // tm,),
        in_specs=[row(D_MODEL), colmajor(1)] + [_const_spec(c.shape) for c in consts],
        out_specs=out_specs,
        compiler_params=pltpu.CompilerParams(
            dimension_semantics=("arbitrary",), vmem_limit_bytes=VMEM_LIMIT),
        name="in_proj",
    )(x2, pos, *consts)


MLA_BUFS = 4
MLA_KEY_TILE = 256
MLA_HEADS_PER_STEP = 4


def _fold_rows(t, op):
    return op(t.reshape(t.shape[0] // SUBLANES, SUBLANES, t.shape[1]), axis=0)


def _mla_body(qt_ref, k_ref, vt_ref, o_ref, s_ref, p_ref, *, tq, nq, heads):
    row = lax.broadcasted_iota(jnp.int32, (tq, tq), 0)
    col = lax.broadcasted_iota(jnp.int32, (tq, tq), 1)
    keep = row <= col
    units = [(qi, hh) for qi in reversed(range(nq)) for hh in range(heads)]
    stats = [dict() for _ in units]

    def tiles(qi):
        below = [slice(lo, min(lo + MLA_KEY_TILE, qi * tq))
                 for lo in range(0, qi * tq, MLA_KEY_TILE)]
        return below + [slice(qi * tq, (qi + 1) * tq)]

    def scores(u):
        qi, hh = units[u]
        lanes = slice(hh * HEAD_PAD, (hh + 1) * HEAD_PAD)
        qh = jnp.concatenate([qt_ref[hh * MLA_QK:(hh + 1) * MLA_QK, qi * tq:(qi + 1) * tq],
                              jnp.zeros((HEAD_PAD - MLA_QK, tq), BF16)], axis=0)
        m8 = None
        for krows in tiles(qi):
            s = jnp.dot(k_ref[krows, lanes], qh, preferred_element_type=F32)
            if krows.start == qi * tq:
                s = jnp.where(keep, s, -jnp.inf)
            s_ref[u % MLA_BUFS, krows, :] = s
            tile_max = _fold_rows(s, jnp.max)
            m8 = tile_max if m8 is None else jnp.maximum(m8, tile_max)
            yield
        stats[u]["m"] = jnp.max(m8, axis=0, keepdims=True)

    def probs(u):
        qi, _ = units[u]
        m = stats[u]["m"]
        l8 = jnp.zeros((SUBLANES, tq), F32)
        for krows in tiles(qi):
            p = jnp.exp2(s_ref[u % MLA_BUFS, krows, :] - m)
            l8 = l8 + _fold_rows(p, jnp.sum)
            p_ref[u % MLA_BUFS, krows, :] = p.astype(BF16)
            yield
        stats[u]["l"] = jnp.sum(l8, axis=0, keepdims=True)

    def values(u):
        qi, hh = units[u]
        kv_len = (qi + 1) * tq
        vrows = slice(hh * MLA_VDIM, (hh + 1) * MLA_VDIM)
        o_t = jnp.dot(vt_ref[vrows, 0:kv_len], p_ref[u % MLA_BUFS, 0:kv_len, :],
                      preferred_element_type=F32)
        o_ref[vrows, qi * tq:(qi + 1) * tq] = (o_t / stats[u]["l"]).astype(BF16)
        yield

    stages = (scores, probs, values)
    for step in range(len(units) + len(stages) - 1):
        live = [stage(step - d) for d, stage in enumerate(stages)
                if 0 <= step - d < len(units)]
        while live:
            live = [g for g in live if next(g, StopIteration) is not StopIteration]


def _mla_attn(qt, k2, vt, B, S, tq):
    heads = MLA_HEADS_PER_STEP
    return pl.pallas_call(
        functools.partial(_mla_body, tq=tq, nq=S // tq, heads=heads),
        out_shape=jax.ShapeDtypeStruct((MLA_WIDTH, B * S), BF16),
        grid=(B, MLA_HEADS // heads),
        in_specs=[pl.BlockSpec((heads * MLA_QK, S), lambda b, p: (p, b)),
                  pl.BlockSpec((S, heads * HEAD_PAD), lambda b, p: (b, p)),
                  pl.BlockSpec((heads * MLA_VDIM, S), lambda b, p: (p, b))],
        out_specs=pl.BlockSpec((heads * MLA_VDIM, S), lambda b, p: (p, b)),
        scratch_shapes=[pltpu.VMEM((MLA_BUFS, S, tq), F32),
                        pltpu.VMEM((MLA_BUFS, S, tq), BF16)],
        compiler_params=pltpu.CompilerParams(
            dimension_semantics=("arbitrary", "arbitrary"), vmem_limit_bytes=VMEM_LIMIT),
        name="mla_attn",
    )(qt, k2, vt)


GLA_BLOCK = 256
GLA_HEADS_PER_STEP = 2


def _gla_body(q_ref, k_ref, v_ref, la_ref, o_ref, oi_ref, qin_ref, u_ref, dec_ref,
              *, seq, heads):
    C, BLK = GLA_CHUNK, GLA_BLOCK
    per_blk = BLK // C
    n_chunks = seq // C
    r = lax.broadcasted_iota(jnp.int32, (BLK, BLK), 0)
    c = lax.broadcasted_iota(jnp.int32, (BLK, BLK), 1)
    causal = (c <= r) & (r // C == c // C)
    tril = jnp.where(causal, 1.0, 0.0).astype(BF16)

    chunk_of_row = lax.broadcasted_iota(jnp.int32, (BLK, GLA_HK), 0) // C
    units = [(blk, hh) for blk in range(seq // BLK) for hh in range(heads)]
    vals = [dict() for _ in units]
    state = [jnp.zeros((GLA_HV, GLA_HK), F32) for _ in range(heads)]

    def kcols(hh):
        return slice(hh * GLA_HK, (hh + 1) * GLA_HK)

    def vcols(hh):
        return slice(hh * GLA_HV, (hh + 1) * GLA_HV)

    def decay(u):
        blk, hh = units[u]
        g = la_ref[blk * BLK:(blk + 1) * BLK, kcols(hh)]
        g_hi = g.astype(BF16)
        g_lo = (g - g_hi.astype(F32)).astype(BF16)
        bb = jnp.dot(tril, jnp.concatenate([g_hi, g_lo], axis=1), preferred_element_type=F32)
        vals[u]["b"] = bb[:, :GLA_HK] + bb[:, GLA_HK:]
        yield

    def scale(u):
        blk, hh = units[u]
        rows = slice(blk * BLK, (blk + 1) * BLK)
        b = vals[u].pop("b")
        lasts = [b[(j + 1) * C - 1:(j + 1) * C, :] for j in range(per_blk)]
        b_last = jnp.concatenate([jnp.broadcast_to(t, (C, GLA_HK)) for t in lasts], axis=0)
        q = q_ref[rows, kcols(hh)]
        k = k_ref[rows, kcols(hh)]
        q_in = (q * jnp.exp(b)).astype(BF16)
        k_in = (k * jnp.exp(-b)).astype(BF16)
        k_st = (k * jnp.exp(b_last - b)).astype(BF16)
        yield
        qin_ref[rows, kcols(hh)] = q_in
        for j in range(per_blk):
            dec_ref[hh * n_chunks + blk * per_blk + j] = jnp.exp(lasts[j])
        zero = jnp.zeros_like(k_st)
        vals[u]["k_bd"] = jnp.concatenate(
            [jnp.where(chunk_of_row == j, k_st, zero) for j in range(per_blk)], axis=1)
        vals[u]["attn"] = lax.dot_general(q_in, k_in, _NT, preferred_element_type=F32)
        yield

    def mix(u):
        blk, hh = units[u]
        rows = slice(blk * BLK, (blk + 1) * BLK)
        attn = jnp.where(causal, vals[u].pop("attn"), 0.0).astype(BF16)
        v = v_ref[rows, vcols(hh)]
        oi_ref[rows, vcols(hh)] = jnp.dot(attn, v, preferred_element_type=F32)
        yield
        u_all = lax.dot_general(v, vals[u].pop("k_bd"), _TN,
                                preferred_element_type=F32)
        for j in range(per_blk):
            u_ref[hh * n_chunks + blk * per_blk + j] = u_all[:, j * GLA_HK:(j + 1) * GLA_HK]
        yield

    def scan(u):
        blk, hh = units[u]
        for n in range(blk * per_blk, (blk + 1) * per_blk):
            crow = slice(n * C, (n + 1) * C)
            st = state[hh]
            o = oi_ref[crow, vcols(hh)] + lax.dot_general(
                qin_ref[crow, kcols(hh)], st.astype(BF16), _NT, preferred_element_type=F32)
            o_ref[crow, vcols(hh)] = _unit_rms(o).astype(BF16)
            state[hh] = st * dec_ref[hh * n_chunks + n] + u_ref[hh * n_chunks + n]
            yield

    stages = (decay, scale, mix, scan)
    for step in range(len(units) + len(stages) - 1):
        live = [stage(step - d) for d, stage in enumerate(stages)
                if 0 <= step - d < len(units)]
        while live:
            live = [g for g in live if next(g, StopIteration) is not StopIteration]


def _gla(q3, k3, v3, la3):
    B, S, _ = q3.shape
    heads = GLA_HEADS_PER_STEP
    n_chunks = S // GLA_CHUNK

    def spec(w):
        return pl.BlockSpec((None, S, heads * w), lambda b, h: (b, 0, h))

    return pl.pallas_call(
        functools.partial(_gla_body, seq=S, heads=heads),
        out_shape=jax.ShapeDtypeStruct((B, S, GLA_DV), BF16),
        grid=(B, GLA_HEADS // heads),
        in_specs=[spec(GLA_HK), spec(GLA_HK), spec(GLA_HV), spec(GLA_HK)],
        out_specs=spec(GLA_HV),
        scratch_shapes=[pltpu.VMEM((S, heads * GLA_HV), F32),
                        pltpu.VMEM((S, heads * GLA_HK), BF16),
                        pltpu.VMEM((heads * n_chunks, GLA_HV, GLA_HK), F32),
                        pltpu.VMEM((heads * n_chunks, 1, GLA_HK), F32)],
        compiler_params=pltpu.CompilerParams(
            dimension_semantics=("arbitrary", "arbitrary"), vmem_limit_bytes=VMEM_LIMIT),
        name="gla",
    )(q3, k3, v3, la3)


def _out_body(x_ref, omt_ref, szmt_ref, og_ref, szg_ref, gm_ref, gg_ref,
              wpm_ref, wpg_ref, wo_ref, gf_ref, o_ref, *, final_norm):
    umt = (omt_ref[...].astype(F32) * szmt_ref[...].astype(F32)).astype(BF16)
    ug = (og_ref[...].astype(F32) * szg_ref[...].astype(F32)).astype(BF16)
    y_mla = lax.dot_general(umt, wpm_ref[...], _TN, preferred_element_type=F32)
    y_gla = jnp.dot(ug, wpg_ref[...], preferred_element_type=F32)
    merged = gm_ref[...].astype(F32) * y_mla + gg_ref[...].astype(F32) * y_gla
    r = x_ref[...] + jnp.dot(merged.astype(BF16), wo_ref[...], preferred_element_type=F32)
    o_ref[...] = _rms(r, gf_ref[...]) if final_norm else r


def _out_proj(x2, omt, szmt, og, szg, gm, gg, wpm, wpg, wo, gf, tm, final_norm):
    T = x2.shape[0]

    def row(w):
        return pl.BlockSpec((tm, w), lambda i: (i, 0))

    def colmajor(r):
        return pl.BlockSpec((r, tm), lambda i: (0, i))

    return pl.pallas_call(
        functools.partial(_out_body, final_norm=final_norm),
        out_shape=jax.ShapeDtypeStruct((T, D_MODEL), F32),
        grid=(T // tm,),
        in_specs=[row(D_MODEL), colmajor(MLA_WIDTH), colmajor(MLA_WIDTH), row(GLA_DV),
                  row(GLA_DV), row(D_MODEL), row(D_MODEL), _const_spec(wpm.shape),
                  _const_spec(wpg.shape), _const_spec(wo.shape), _const_spec(gf.shape)],
        out_specs=row(D_MODEL),
        compiler_params=pltpu.CompilerParams(
            dimension_semantics=("arbitrary",), vmem_limit_bytes=VMEM_LIMIT),
        name="out_proj",
    )(x2, omt, szmt, og, szg, gm, gg, wpm, wpg, wo, gf)


def _pad_heads(w, width):
    r = w.shape[0]
    w = w.reshape(r, MLA_HEADS, width)
    w = jnp.pad(w, ((0, 0), (0, 0), (0, HEAD_PAD - width)))
    return w.reshape(r, QPAD)


def kernel(x, positions, g_in, w_in, g_q, w_uq, g_kv, w_ukv, w_gla_gate, b_gla_gate,
           g_gla, w_proj_mla, w_proj_gla, w_out, g_final):
    B, S, D = x.shape
    T = B * S
    depth = w_in.shape[0]
    assert D == D_MODEL and w_in.shape[1:] == (D_MODEL, sum(SPLITS))
    assert T % OUT_TOKEN_TILE == 0 and OUT_TOKEN_TILE % IN_TOKEN_TILE == 0
    assert S % MLA_QUERY_BLOCK == 0 and S % GLA_BLOCK == 0

    freq = (ROPE_THETA ** (-jnp.arange(ROPE_HALF, dtype=F32) / ROPE_HALF)).reshape(ROPE_HALF, 1)
    pos = positions.astype(F32).reshape(1, T)

    x2 = x.reshape(T, D)
    for l in range(depth):
        wt = (w_in[l] * g_in[l][:, None]).T.astype(BF16)
        lo = _ROWS["alr"][0]
        wlr = jnp.concatenate(
            [wt[:LOW_RANK], wt[lo:lo + GLA_GATE_RANK],
             jnp.zeros((LANES - GLA_GATE_RANK, D), BF16)], axis=0)
        wuqt = (w_uq[l] * g_q[l][:, None] * (MLA_QK ** -0.5 * LOG2E)).T.astype(BF16)
        wukv = (w_ukv[l] * g_kv[l][:, None]).reshape(
            MLA_KV_RANK, MLA_HEADS, MLA_NOPE + MLA_VDIM)
        wuk = _pad_heads(wukv[:, :, :MLA_NOPE].reshape(MLA_KV_RANK, -1), MLA_NOPE).astype(BF16)
        wuvt = wukv[:, :, MLA_NOPE:].reshape(MLA_KV_RANK, MLA_WIDTH).T.astype(BF16)
        wgg = jnp.pad(w_gla_gate[l], ((0, LANES - GLA_GATE_RANK), (0, 0))).astype(BF16)
        wpg = (w_proj_gla[l] * jnp.tile(g_gla[l], GLA_HEADS)[:, None]).astype(BF16)

        (qt, k, vt, szmt, qg, kg, vg, la, szg, gm, gg) = _in_proj(
            x2, pos, freq, wt, wlr, wuqt, wuk, wuvt, wgg, b_gla_gate[l].reshape(1, -1),
            IN_TOKEN_TILE)

        omt = _mla_attn(qt, k, vt, B, S, tq=MLA_QUERY_BLOCK)
        o_gla = _gla(qg.reshape(B, S, GLA_DK), kg.reshape(B, S, GLA_DK),
                     vg.reshape(B, S, GLA_DV), la.reshape(B, S, GLA_DK))

        x2 = _out_proj(x2, omt, szmt, o_gla.reshape(T, GLA_DV), szg,
                       gm, gg, w_proj_mla[l].astype(BF16), wpg,
                       w_out[l].astype(BF16), g_final.reshape(1, D), OUT_TOKEN_TILE,
                       final_norm=(l == depth - 1))
    return x2.reshape(B, S, D)
```
